```python
import jax
import jax.numpy as jnp
from jax import lax
import numpy as np

D_MODEL = 1024
BATCH = 2
SEQ = 8192
DEPTH = 4
DEC_BATCH = 128
DEC_SEQ = 4
PAST_LEN = 2048
PAGE_SIZE = 128

HEAD_DIM = 64
ROPE_THETA = 10000.0
RMS_EPS = 1e-6
Q_BLOCK = 128
NEG_INF = -1e30
ATTN_SCALE = HEAD_DIM ** -0.5

DIL_WINDOWS = (128, 512, 2048)
DIL_RATES = (1, 4, 16)
N_DIL_GROUPS = len(DIL_WINDOWS)
DIL_HEADS = D_MODEL // 128
A_IN_COLS = N_DIL_GROUPS * 3 * DIL_HEADS * HEAD_DIM
A_OUT_ROWS = DIL_HEADS * HEAD_DIM

NSA_HEADS = D_MODEL // HEAD_DIM
NSA_KV_GROUPS = 2
NSA_REP = NSA_HEADS // NSA_KV_GROUPS
CMP_LEN = 32
CMP_STRIDE = 16
CMP_HIDDEN = 256
SEL_LEN = 64
SEL_TOPK = 16
SLD_WINDOW = 512
FORCED_SCORE = 1e9
NSA_Q_COLS = NSA_HEADS * HEAD_DIM
NSA_KV_COLS = 3 * 2 * NSA_KV_GROUPS * HEAD_DIM
NSA_GATE_COLS = 3 * NSA_HEADS
B_IN_COLS = NSA_Q_COLS + NSA_KV_COLS + NSA_GATE_COLS

N_EXPERT_GROUPS = 4
EXPERTS_PER_GROUP = 8
N_EXPERTS = N_EXPERT_GROUPS * EXPERTS_PER_GROUP
TOP_K_IN_GROUP = 2
D_EXPERT = D_MODEL // 4
MOE_BLOCK = 1024

PLE_DIM = 256

kernel_name = "hybrid_dilated_nsa_hmoe_decode_step"


def rms_norm(x, g):
    xf = x.astype(jnp.float32)
    y = xf * lax.rsqrt(jnp.mean(xf * xf, axis=-1, keepdims=True) + RMS_EPS)
    return (y * g).astype(x.dtype)


def rope(x, pos):
    half = x.shape[-1] // 2
    inv = ROPE_THETA ** (-jnp.arange(half, dtype=jnp.float32) / half)
    ang = pos.astype(jnp.float32)[:, None] * inv[None, :]
    shape = (pos.shape[0],) + (1,) * (x.ndim - 3) + (half,)
    cos = jnp.cos(ang).reshape(shape)
    sin = jnp.sin(ang).reshape(shape)
    xf = x.astype(jnp.float32)
    x1, x2 = xf[..., :half], xf[..., half:]
    return jnp.concatenate([x1 * cos - x2 * sin, x2 * cos + x1 * sin], axis=-1).astype(x.dtype)


def masked_softmax(s, mask):
    s = jnp.where(mask, s, NEG_INF)
    m = jnp.max(s, axis=-1, keepdims=True)
    e = jnp.where(mask, jnp.exp(s - m), 0.0)
    den = jnp.sum(e, axis=-1, keepdims=True)
    den = jnp.where(den > 0, den, 1.0)
    return e / den, (m + jnp.log(den))[..., 0]


def over_query_blocks(fn, n_q):
    qb = Q_BLOCK if n_q % Q_BLOCK == 0 else n_q
    out = lax.map(lambda c: fn(c * qb, qb), jnp.arange(n_q // qb, dtype=jnp.int32))
    out = jnp.moveaxis(out, 0, 1)
    return out.reshape((out.shape[0], n_q) + out.shape[3:])


def dilated_group_attend(q, k_src, v_src, qpos, kpos0, window, rate):
    n_keys = window // rate + 1
    kp = qpos[:, None] - rate * jnp.arange(n_keys, dtype=jnp.int32)[None, :]
    idx = jnp.clip(kp - kpos0, 0, k_src.shape[1] - 1)
    kg = jnp.take(k_src, idx, axis=1)
    vg = jnp.take(v_src, idx, axis=1)
    s = jnp.einsum('bqhd,bqjhd->bhqj', q, kg, preferred_element_type=jnp.float32) * ATTN_SCALE
    p, lse = masked_softmax(s, (kp >= 0)[None, None])
    o = jnp.einsum('bhqj,bqjhd->bqhd', p.astype(v_src.dtype), vg)
    return o, lse


def mixer_a(hn, pos0, past, w_in, w_out):
    bsz, n_q, _ = hn.shape
    pos = pos0 + jnp.arange(n_q, dtype=jnp.int32)
    qkv = (hn @ w_in).reshape(bsz, n_q, N_DIL_GROUPS, 3, DIL_HEADS, HEAD_DIM)
    q = rope(qkv[:, :, :, 0], pos)
    rows = jnp.stack([rope(qkv[:, :, :, 1], pos), qkv[:, :, :, 2]], axis=3)
    k_srcs, v_srcs, kpos0s, new_bufs = [], [], [], []
    for g in range(N_DIL_GROUPS):
        if past is None:
            src = rows[:, :, g]
            kpos0 = pos0
            keep = min(DIL_WINDOWS[g], n_q)
        else:
            src = jnp.concatenate([past[g], rows[:, :, g]], axis=1)
            kpos0 = pos0 - past[g].shape[1]
            keep = past[g].shape[1]
        new_bufs.append(src[:, src.shape[1] - keep:])
        k_srcs.append(src[:, :, 0])
        v_srcs.append(src[:, :, 1])
        kpos0s.append(kpos0)

    def block(start, qb):
        qpos = pos0 + start + jnp.arange(qb, dtype=jnp.int32)
        q_blk = lax.dynamic_slice_in_dim(q, start, qb, axis=1)
        outs, lses = [], []
        for g in range(N_DIL_GROUPS):
            o, lse = dilated_group_attend(q_blk[:, :, g], k_srcs[g], v_srcs[g], qpos, kpos0s[g],
                                          DIL_WINDOWS[g], DIL_RATES[g])
            outs.append(o)
            lses.append(lse)
        wts = jax.nn.softmax(jnp.stack(lses, axis=0), axis=0)
        wts = jnp.transpose(wts, (0, 1, 3, 2))[..., None]
        o = jnp.sum(wts * jnp.stack(outs, axis=0).astype(jnp.float32), axis=0)
        return o.astype(hn.dtype)

    o = over_query_blocks(block, n_q)
    return o.reshape(bsz, n_q, A_OUT_ROWS) @ w_out, new_bufs


def mixer_b(hn, pos0, past_rows, past_win, w_in, w_out, cmp_pe, cmp_w1, cmp_w2):
    bsz, n_q, _ = hn.shape
    dt = hn.dtype
    pos = pos0 + jnp.arange(n_q, dtype=jnp.int32)
    proj = hn @ w_in
    q = rope(proj[..., :NSA_Q_COLS].reshape(bsz, n_q, NSA_HEADS, HEAD_DIM), pos)
    kv = proj[..., NSA_Q_COLS:NSA_Q_COLS + NSA_KV_COLS].reshape(bsz, n_q, 3, 2, NSA_KV_GROUPS, HEAD_DIM)
    gates = jax.nn.sigmoid(proj[..., NSA_Q_COLS + NSA_KV_COLS:].astype(jnp.float32)).reshape(bsz, n_q, 3, NSA_HEADS)
    kv = jnp.stack([rope(kv[:, :, :, 0], pos), kv[:, :, :, 1]], axis=3)
    new_rows = kv[:, :, :2].reshape(bsz, n_q, 4, NSA_KV_GROUPS, HEAD_DIM)
    new_win = kv[:, :, 2]
    if past_rows is None:
        rows, win_src, win_kpos0 = new_rows, new_win, pos0
        keep = min(SLD_WINDOW, n_q)
    else:
        rows = jnp.concatenate([past_rows, new_rows], axis=1)
        win_src = jnp.concatenate([past_win, new_win], axis=1)
        win_kpos0 = pos0 - past_win.shape[1]
        keep = past_win.shape[1]
    new_win_buf = win_src[:, win_src.shape[1] - keep:]
    n_k = rows.shape[1]

    n_cmp = (n_k - CMP_LEN) // CMP_STRIDE + 1
    starts = jnp.arange(n_cmp, dtype=jnp.int32) * CMP_STRIDE
    cmp_end = starts + CMP_LEN - 1
    idx = starts[:, None] + jnp.arange(CMP_LEN, dtype=jnp.int32)[None, :]
    blocks = jnp.take(rows[:, :, :2], idx, axis=1)
    blocks = blocks + jnp.transpose(cmp_pe, (1, 0, 2))[None, None, :, :, None, :]
    flat = jnp.transpose(blocks, (0, 1, 3, 4, 2, 5)).reshape(bsz, n_cmp, 2, NSA_KV_GROUPS, CMP_LEN * HEAD_DIM)
    hid = jax.nn.gelu(jnp.einsum('bnkgf,kfh->bnkgh', flat, cmp_w1))
    ckv = jnp.einsum('bnkgh,khd->bnkgd', hid, cmp_w2)
    k_cmp, v_cmp = ckv[:, :, 0], ckv[:, :, 1]

    n_sel = -(-n_k // SEL_LEN)
    sel = jnp.pad(rows[:, :, 2:], ((0, 0), (0, n_sel * SEL_LEN - n_k), (0, 0), (0, 0), (0, 0)))
    sel = jnp.transpose(sel.reshape(bsz, n_sel, SEL_LEN, 2, NSA_KV_GROUPS, HEAD_DIM), (0, 4, 1, 2, 3, 5))
    blk_id = jnp.arange(n_sel, dtype=jnp.int32)
    overlap = ((starts[:, None] < (blk_id[None, :] + 1) * SEL_LEN)
               & (cmp_end[:, None] >= blk_id[None, :] * SEL_LEN)).astype(jnp.float32)
    top_k = min(SEL_TOPK, n_sel)
    bi = jnp.arange(bsz)[:, None, None, None]
    gi = jnp.arange(NSA_KV_GROUPS)[None, :, None, None]

    win_pad = jnp.pad(win_src, ((0, 0), (SLD_WINDOW, 0), (0, 0), (0, 0), (0, 0)))
    off = pos0 - win_kpos0

    def block(start, qb):
        qpos = pos0 + start + jnp.arange(qb, dtype=jnp.int32)
        qg = lax.dynamic_slice_in_dim(q, start, qb, axis=1).reshape(bsz, qb, NSA_KV_GROUPS, NSA_REP, HEAD_DIM)
        s = jnp.einsum('bqgrd,bngd->bgrqn', qg, k_cmp, preferred_element_type=jnp.float32) * ATTN_SCALE
        p_cmp, _ = masked_softmax(s, (cmp_end[None, :] <= qpos[:, None])[None, None, None])
        o_cmp = jnp.einsum('bgrqn,bngd->bqgrd', p_cmp.astype(dt), v_cmp)
        imp = jnp.einsum('bgrqn,nj->bgqj', p_cmp, overlap)
        cur = qpos // SEL_LEN
        eligible = blk_id[None, :] <= cur[:, None]
        forced = (blk_id[None, :] == 0) | (blk_id[None, :] == cur[:, None]) | (blk_id[None, :] == cur[:, None] - 1)
        score = jnp.where(eligible, jnp.where(forced, FORCED_SCORE, imp), NEG_INF)
        vals, sel_idx = lax.top_k(score, top_k)
        kvsel = sel[bi, gi, sel_idx]
        tok = sel_idx[..., None] * SEL_LEN + jnp.arange(SEL_LEN, dtype=jnp.int32)
        m_sel = (vals > NEG_INF * 0.5)[..., None] & (tok <= qpos[None, None, :, None, None])
        s = jnp.einsum('bqgrd,bgqkld->bgrqkl', qg, kvsel[..., 0, :], preferred_element_type=jnp.float32) * ATTN_SCALE
        s = s.reshape(bsz, NSA_KV_GROUPS, NSA_REP, qb, top_k * SEL_LEN)
        p_sel, _ = masked_softmax(s, m_sel.reshape(bsz, NSA_KV_GROUPS, 1, qb, top_k * SEL_LEN))
        p_sel = p_sel.reshape(bsz, NSA_KV_GROUPS, NSA_REP, qb, top_k, SEL_LEN)
        o_sel = jnp.einsum('bgrqkl,bgqkld->bqgrd', p_sel.astype(dt), kvsel[..., 1, :])
        wk = lax.dynamic_slice_in_dim(win_pad, off + start, qb + SLD_WINDOW, axis=1)
        kpos = pos0 + start - SLD_WINDOW + jnp.arange(qb + SLD_WINDOW, dtype=jnp.int32)
        dist = qpos[:, None] - kpos[None, :]
        m_win = (kpos[None, :] >= 0) & (dist >= 0) & (dist <= SLD_WINDOW)
        s = jnp.einsum('bqgrd,bkgd->bgrqk', qg, wk[:, :, 0], preferred_element_type=jnp.float32) * ATTN_SCALE
        p_win, _ = masked_softmax(s, m_win[None, None, None])
        o_win = jnp.einsum('bgrqk,bkgd->bqgrd', p_win.astype(dt), wk[:, :, 1])
        g = lax.dynamic_slice_in_dim(gates, start, qb, axis=1).reshape(bsz, qb, 3, NSA_KV_GROUPS, NSA_REP, 1)
        o = (g[:, :, 0] * o_cmp.astype(jnp.float32) + g[:, :, 1] * o_sel.astype(jnp.float32)
             + g[:, :, 2] * o_win.astype(jnp.float32))
        return o.reshape(bsz, qb, NSA_HEADS, HEAD_DIM).astype(dt)

    o = over_query_blocks(block, n_q)
    return o.reshape(bsz, n_q, NSA_Q_COLS) @ w_out, new_rows, new_win_buf


def hier_moe(hn, w_group, b_group, w_expert, b_expert, w_gate, w_up, w_down):
    bsz, n_q, d = hn.shape
    x = hn.reshape(bsz * n_q, d)
    n_tok = x.shape[0]
    blk = MOE_BLOCK if n_tok % MOE_BLOCK == 0 else n_tok

    def block(xb):
        rows = jnp.arange(xb.shape[0])
        lg = jnp.einsum('nd,dg->ng', xb, w_group, preferred_element_type=jnp.float32) + b_group
        grp = jnp.argmax(lg, axis=-1)
        p_grp = jax.nn.softmax(lg, axis=-1)[rows, grp]
        le = (jnp.einsum('nd,de->ne', xb, w_expert, preferred_element_type=jnp.float32)
              + b_expert).reshape(-1, N_EXPERT_GROUPS, EXPERTS_PER_GROUP)
        top_v, top_i = lax.top_k(le[rows, grp], TOP_K_IN_GROUP)
        w_sel = jax.nn.softmax(top_v, axis=-1) * p_grp[:, None]
        e_id = grp[:, None] * EXPERTS_PER_GROUP + top_i
        combine = jnp.sum(jax.nn.one_hot(e_id, N_EXPERTS, dtype=jnp.float32) * w_sel[..., None], axis=1)
        hg = jnp.einsum('nd,edf->nef', xb, w_gate)
        hu = jnp.einsum('nd,edf->nef', xb, w_up)
        a = (jax.nn.silu(hg.astype(jnp.float32)) * hu.astype(jnp.float32) * combine[..., None]).astype(xb.dtype)
        return jnp.einsum('nef,efd->nd', a, w_down)

    y = lax.map(block, x.reshape(n_tok // blk, blk, d))
    return y.reshape(bsz, n_q, d)


def trunk(x, p, pos0, a_past, b_past, norm_mix, norm_ffn, norm_ple, norm_final,
          a_w_in, a_w_out, b_w_in, b_w_out, b_cmp_pe, b_cmp_w1, b_cmp_w2,
          moe_w_group, moe_b_group, moe_w_expert, moe_b_expert, moe_w_gate, moe_w_up, moe_w_down,
          ple_w_proj, ple_w_gate):
    h = x
    a_bufs, b_rows, b_wins = [], [], []
    for i in range(DEPTH):
        li = i // 2
        hn = rms_norm(h, norm_mix[i])
        if i % 2 == 0:
            mix, bufs = mixer_a(hn, pos0, None if a_past is None else a_past[li], a_w_in[li], a_w_out[li])
            a_bufs.append(bufs)
        else:
            past_rows, past_win = (None, None) if b_past is None else b_past[li]
            mix, rows, win = mixer_b(hn, pos0, past_rows, past_win, b_w_in[li], b_w_out[li],
                                     b_cmp_pe[li], b_cmp_w1[li], b_cmp_w2[li])
            b_rows.append(rows)
            b_wins.append(win)
        h = h + mix
        h = h + hier_moe(rms_norm(h, norm_ffn[i]), moe_w_group[i], moe_b_group[i], moe_w_expert[i],
                         moe_b_expert[i], moe_w_gate[i], moe_w_up[i], moe_w_down[i])
        gate = jax.nn.sigmoid(rms_norm(h, norm_ple[i]) @ ple_w_gate[i])
        h = h + gate * (p[i] @ ple_w_proj[i])
    y = rms_norm(h, norm_final)
    a_state = [jnp.stack([bufs[g] for bufs in a_bufs], axis=0) for g in range(N_DIL_GROUPS)]
    return y, a_state, jnp.stack(b_rows, axis=0), jnp.stack(b_wins, axis=0)


def setup_inputs(seed: int = 0) -> dict:
    key = jax.random.key(seed)
    ks = iter(jax.random.split(key, 40))

    def nrm(shape, scale):
        return jax.random.normal(next(ks), shape, jnp.float32) * scale

    n_a = (DEPTH + 1) // 2
    n_b = DEPTH // 2
    n_pages = PAST_LEN // PAGE_SIZE
    n_phys = (DEC_BATCH * n_pages * 5) // 4
    page_table = jax.random.permutation(next(ks), n_phys)[:DEC_BATCH * n_pages].reshape(DEC_BATCH, n_pages).astype(jnp.int32)
    return {
        "x_prompt": nrm((BATCH, SEQ, D_MODEL), 1.0),
        "x_sample": nrm((DEC_BATCH, DEC_SEQ, D_MODEL), 1.0),
        "p_prompt": nrm((DEPTH, BATCH, SEQ, PLE_DIM), 1.0),
        "p_sample": nrm((DEPTH, DEC_BATCH, DEC_SEQ, PLE_DIM), 1.0),
        "cache_a_w128": nrm((n_a, DEC_BATCH, min(DIL_WINDOWS[0], PAST_LEN), 2, DIL_HEADS, HEAD_DIM), 1.0),
        "cache_a_w512": nrm((n_a, DEC_BATCH, min(DIL_WINDOWS[1], PAST_LEN), 2, DIL_HEADS, HEAD_DIM), 1.0),
        "cache_a_w2048": nrm((n_a, DEC_BATCH, min(DIL_WINDOWS[2], PAST_LEN), 2, DIL_HEADS, HEAD_DIM), 1.0),
        "cache_b_kv": nrm((n_b, n_phys, PAGE_SIZE, 4, NSA_KV_GROUPS, HEAD_DIM), 1.0),
        "cache_b_win": nrm((n_b, DEC_BATCH, min(SLD_WINDOW, PAST_LEN), 2, NSA_KV_GROUPS, HEAD_DIM), 1.0),
        "page_table": page_table,
        "norm_mix": 1.0 + nrm((DEPTH, D_MODEL), 0.02),
        "norm_ffn": 1.0 + nrm((DEPTH, D_MODEL), 0.02),
        "norm_ple": 1.0 + nrm((DEPTH, D_MODEL), 0.02),
        "norm_final": 1.0 + nrm((D_MODEL,), 0.02),
        "a_w_in": nrm((n_a, D_MODEL, A_IN_COLS), D_MODEL ** -0.5),
        "a_w_out": nrm((n_a, A_OUT_ROWS, D_MODEL), A_OUT_ROWS ** -0.5),
        "b_w_in": nrm((n_b, D_MODEL, B_IN_COLS), D_MODEL ** -0.5),
        "b_w_out": nrm((n_b, NSA_Q_COLS, D_MODEL), NSA_Q_COLS ** -0.5),
        "b_cmp_pe": nrm((n_b, 2, CMP_LEN, HEAD_DIM), 0.02),
        "b_cmp_w1": nrm((n_b, 2, CMP_LEN * HEAD_DIM, CMP_HIDDEN), (CMP_LEN * HEAD_DIM) ** -0.5),
        "b_cmp_w2": nrm((n_b, 2, CMP_HIDDEN, HEAD_DIM), CMP_HIDDEN ** -0.5),
        "moe_w_group": nrm((DEPTH, D_MODEL, N_EXPERT_GROUPS), D_MODEL ** -0.5),
        "moe_b_group": nrm((DEPTH, N_EXPERT_GROUPS), 0.01),
        "moe_w_expert": nrm((DEPTH, D_MODEL, N_EXPERTS), D_MODEL ** -0.5),
        "moe_b_expert": nrm((DEPTH, N_EXPERTS), 0.01),
        "moe_w_gate": nrm((DEPTH, N_EXPERTS, D_MODEL, D_EXPERT), D_MODEL ** -0.5),
        "moe_w_up": nrm((DEPTH, N_EXPERTS, D_MODEL, D_EXPERT), D_MODEL ** -0.5),
        "moe_w_down": nrm((DEPTH, N_EXPERTS, D_EXPERT, D_MODEL), D_EXPERT ** -0.5),
        "ple_w_proj": nrm((DEPTH, PLE_DIM, D_MODEL), PLE_DIM ** -0.5),
        "ple_w_gate": nrm((DEPTH, D_MODEL, D_MODEL), D_MODEL ** -0.5),
    }


def reference(x_prompt, x_sample, p_prompt, p_sample, cache_a_w128, cache_a_w512, cache_a_w2048,
              cache_b_kv, cache_b_win, page_table, norm_mix, norm_ffn, norm_ple, norm_final,
              a_w_in, a_w_out, b_w_in, b_w_out, b_cmp_pe, b_cmp_w1, b_cmp_w2,
              moe_w_group, moe_b_group, moe_w_expert, moe_b_expert, moe_w_gate, moe_w_up, moe_w_down,
              ple_w_proj, ple_w_gate):
    y_p, a_p, brows_p, bwin_p = trunk(
        x_prompt, p_prompt, 0, None, None, norm_mix, norm_ffn, norm_ple, norm_final,
        a_w_in, a_w_out, b_w_in, b_w_out, b_cmp_pe, b_cmp_w1, b_cmp_w2,
        moe_w_group, moe_b_group, moe_w_expert, moe_b_expert, moe_w_gate, moe_w_up, moe_w_down,
        ple_w_proj, ple_w_gate)

    n_pages = page_table.shape[1]
    page = cache_b_kv.shape[2]
    past_len = n_pages * page
    dec_b = page_table.shape[0]
    a_past = [[cache_a_w128[l], cache_a_w512[l], cache_a_w2048[l]] for l in range(cache_a_w128.shape[0])]
    b_past = [(cache_b_kv[l][page_table].reshape(dec_b, past_len, 4, NSA_KV_GROUPS, HEAD_DIM), cache_b_win[l])
              for l in range(cache_b_kv.shape[0])]
    y_s, a_s, brows_s, bwin_s = trunk(
        x_sample, p_sample, past_len, a_past, b_past, norm_mix, norm_ffn, norm_ple, norm_final,
        a_w_in, a_w_out, b_w_in, b_w_out, b_cmp_pe, b_cmp_w1, b_cmp_w2,
        moe_w_group, moe_b_group, moe_w_expert, moe_b_expert, moe_w_gate, moe_w_up, moe_w_down,
        ple_w_proj, ple_w_gate)
    return (y_p, y_s, a_p[0], a_p[1], a_p[2], brows_p, bwin_p, a_s[0], a_s[1], a_s[2], brows_s, bwin_s)
```

```python
import functools

import numpy as np
import jax
import jax.numpy as jnp
from jax import lax
from jax.experimental import pallas as pl
from jax.experimental.pallas import tpu as pltpu

F32 = jnp.float32
BF16 = jnp.bfloat16

HEAD_DIM = 64
HALF = HEAD_DIM // 2
ROPE_THETA = 10000.0
RMS_EPS = 1e-6
NEG_INF = -1e30
ATTN_SCALE = HEAD_DIM ** -0.5
LANES = 128
Q_BLK = 128

DIL_WINDOWS = (128, 512, 2048)
DIL_RATES = (1, 4, 16)
DIL_HEADS = 8
DIL_W = DIL_HEADS * HEAD_DIM

NSA_HEADS = 16
NSA_REP = 8
CMP_LEN = 32
CMP_STRIDE = 16
CMP_HIDDEN = 256
SEL_LEN = 64
SEL_TOPK = 16
SLD_WINDOW = 512
FORCED_SCORE = 1e9
B_COLS = 2048
B_Q = 1024
B_KCMP, B_VCMP, B_KSLC, B_VSLC, B_KWIN, B_VWIN, B_GATE = (1024, 1152, 1280, 1408, 1536, 1664, 1792)

N_GROUPS = 4
EXP_PER_GROUP = 8
N_EXPERTS = 32

VMEM_LIMIT = 56 * 1024 * 1024


def _cp(sem, vmem=None):
    return pltpu.CompilerParams(dimension_semantics=sem, vmem_limit_bytes=vmem)


def _nt(a, b):
    return lax.dot_general(a, b, (((1,), (1,)), ((), ())), preferred_element_type=F32)


def _dot(a, b):
    return jnp.dot(a, b, preferred_element_type=F32)


def _iota(shape, dim):
    return lax.broadcasted_iota(jnp.int32, shape, dim)


def _norm_proj_kernel(x_ref, g_ref, w_ref, rc_ref, sc_ref, c_ref, s1_ref, s2_ref,
                      o32_ref, o16_ref, xn_ref, *, tn):
    @pl.when(pl.program_id(1) == 0)
    def _():
        x = x_ref[...]
        var = jnp.mean(x * x, axis=-1, keepdims=True)
        xn_ref[...] = (x * lax.rsqrt(var + RMS_EPS) * g_ref[...]).astype(BF16)

    y = _dot(xn_ref[...], w_ref[...])
    reps = tn // LANES
    rc = rc_ref[...]
    c = 1.0 + rc * (jnp.tile(c_ref[...], (1, reps)) - 1.0)
    s1 = rc * jnp.tile(s1_ref[...], (1, reps))
    s2 = rc * jnp.tile(s2_ref[...], (1, reps))
    y = y * c + pltpu.roll(y, tn - HALF, 1) * s1 + pltpu.roll(y, HALF, 1) * s2
    o32_ref[...] = y
    o16_ref[...] = (y * sc_ref[...]).astype(BF16)


def norm_proj(x, g, w16, ropecol, colscale, rope_c, rope_s1, rope_s2, *, tm=512, tn=512):
    n, d = x.shape
    ncols = w16.shape[1]
    assert n % tm == 0 and ncols % tn == 0
    return pl.pallas_call(
        functools.partial(_norm_proj_kernel, tn=tn),
        grid=(n // tm, ncols // tn),
        in_specs=[
            pl.BlockSpec((tm, d), lambda i, j: (i, 0)),
            pl.BlockSpec((1, d), lambda i, j: (0, 0)),
            pl.BlockSpec((d, tn), lambda i, j: (0, j)),
            pl.BlockSpec((1, tn), lambda i, j: (0, j)),
            pl.BlockSpec((1, tn), lambda i, j: (0, j)),
            pl.BlockSpec((tm, LANES), lambda i, j: (i, 0)),
            pl.BlockSpec((tm, LANES), lambda i, j: (i, 0)),
            pl.BlockSpec((tm, LANES), lambda i, j: (i, 0)),
        ],
        out_specs=[pl.BlockSpec((tm, tn), lambda i, j: (i, j)),
                   pl.BlockSpec((tm, tn), lambda i, j: (i, j))],
        out_shape=[jax.ShapeDtypeStruct((n, ncols), F32),
                   jax.ShapeDtypeStruct((n, ncols), BF16)],
        scratch_shapes=[pltpu.VMEM((tm, d), BF16)],
        compiler_params=_cp(("parallel", "arbitrary")),
        name="norm_proj",
    )(x, g.reshape(1, d), w16, ropecol, colscale, rope_c, rope_s1, rope_s2)


def _out_proj_kernel(h_ref, o_ref, w_ref, out_ref):
    out_ref[...] = h_ref[...] + _dot(o_ref[...], w_ref[...])


def out_proj(h, o16, w16, *, tm=512):
    n, d = h.shape
    k = o16.shape[1]
    return pl.pallas_call(
        _out_proj_kernel,
        grid=(n // tm,),
        in_specs=[pl.BlockSpec((tm, d), lambda i: (i, 0)),
                  pl.BlockSpec((tm, k), lambda i: (i, 0)),
                  pl.BlockSpec((k, d), lambda i: (0, 0))],
        out_specs=pl.BlockSpec((tm, d), lambda i: (i, 0)),
        out_shape=jax.ShapeDtypeStruct((n, d), F32),
        compiler_params=_cp(("parallel",)),
        name="out_proj",
    )(h, o16, w16)


def _moe_kernel(h_ref, g_ref, wr_ref, br_ref, wg_ref, wu_ref, wd_ref, out_ref,
                xn_ref, comb_ref, acc_ref):
    grp = pl.program_id(1)

    @pl.when(grp == 0)
    def _():
        x = h_ref[...]
        var = jnp.mean(x * x, axis=-1, keepdims=True)
        xn = x * lax.rsqrt(var + RMS_EPS) * g_ref[...]
        xn_ref[...] = xn.astype(BF16)
        logit = jnp.dot(xn, wr_ref[...], preferred_element_type=F32,
                        precision=lax.Precision.HIGHEST) + br_ref[...]
        lane = _iota(logit.shape, 1)
        is_grp = lane < N_GROUPS
        lg = jnp.where(is_grp, logit, -jnp.inf)
        gmax = jnp.max(lg, axis=-1, keepdims=True)
        gidx = jnp.min(jnp.where(lg == gmax, lane, LANES), axis=-1, keepdims=True)
        p_grp = 1.0 / jnp.sum(jnp.where(is_grp, jnp.exp(logit - gmax), 0.0), axis=-1, keepdims=True)
        lo = N_GROUPS + gidx * EXP_PER_GROUP
        in_grp = (lane >= lo) & (lane < lo + EXP_PER_GROUP)
        le = jnp.where(in_grp, logit, -jnp.inf)
        v1 = jnp.max(le, axis=-1, keepdims=True)
        i1 = jnp.min(jnp.where(le == v1, lane, LANES), axis=-1, keepdims=True)
        le2 = jnp.where(lane == i1, -jnp.inf, le)
        v2 = jnp.max(le2, axis=-1, keepdims=True)
        i2 = jnp.min(jnp.where(le2 == v2, lane, LANES), axis=-1, keepdims=True)
        e2 = jnp.exp(v2 - v1)
        w1 = p_grp / (1.0 + e2)
        w2 = p_grp * e2 / (1.0 + e2)
        comb = jnp.where(lane == i1, w1, jnp.where(lane == i2, w2, 0.0))
        for gg in range(N_GROUPS):
            sel = jnp.where(gidx == gg, comb, 0.0)
            comb_ref[gg] = pltpu.roll(sel, LANES - (N_GROUPS + gg * EXP_PER_GROUP), 1)
        acc_ref[...] = jnp.zeros_like(acc_ref)

    xn = xn_ref[...]
    comb = comb_ref[grp]
    acc = acc_ref[...]
    for k in range(EXP_PER_GROUP):
        hg = _dot(xn, wg_ref[k])
        hu = _dot(xn, wu_ref[k])
        a = (hg / (1.0 + jnp.exp(-hg))) * hu * comb[:, k:k + 1]
        acc = acc + _dot(a.astype(BF16), wd_ref[k])
    acc_ref[...] = acc

    @pl.when(grp == N_GROUPS - 1)
    def _():
        out_ref[...] = h_ref[...] + acc_ref[...]


def moe(h, g, w_route, b_route, wg16, wu16, wd16, *, tm=512):
    n, d = h.shape
    f = wg16.shape[2]
    return pl.pallas_call(
        _moe_kernel,
        grid=(n // tm, N_GROUPS),
        in_specs=[
            pl.BlockSpec((tm, d), lambda i, e: (i, 0)),
            pl.BlockSpec((1, d), lambda i, e: (0, 0)),
            pl.BlockSpec((d, LANES), lambda i, e: (0, 0)),
            pl.BlockSpec((1, LANES), lambda i, e: (0, 0)),
            pl.BlockSpec((EXP_PER_GROUP, d, f), lambda i, e: (e, 0, 0)),
            pl.BlockSpec((EXP_PER_GROUP, d, f), lambda i, e: (e, 0, 0)),
            pl.BlockSpec((EXP_PER_GROUP, f, d), lambda i, e: (e, 0, 0)),
        ],
        out_specs=pl.BlockSpec((tm, d), lambda i, e: (i, 0)),
        out_shape=jax.ShapeDtypeStruct((n, d), F32),
        scratch_shapes=[pltpu.VMEM((tm, d), BF16),
                        pltpu.VMEM((N_GROUPS, tm, LANES), F32),
                        pltpu.VMEM((tm, d), F32)],
        compiler_params=_cp(("parallel", "arbitrary"), VMEM_LIMIT),
        name="moe",
    )(h, g.reshape(1, d), w_route, b_route, wg16, wu16, wd16)


def _ple_kernel(h_ref, g_ref, p_ref, wgate_ref, wproj_ref, out_ref):
    x = h_ref[...]
    var = jnp.mean(x * x, axis=-1, keepdims=True)
    xn = (x * lax.rsqrt(var + RMS_EPS) * g_ref[...]).astype(BF16)
    gate = 1.0 / (1.0 + jnp.exp(-_dot(xn, wgate_ref[...])))
    out_ref[...] = x + gate * _dot(p_ref[...].astype(BF16), wproj_ref[...])


def ple(h, g, p, wgate16, wproj16, *, tm=512):
    n, d = h.shape
    pd = p.shape[1]
    return pl.pallas_call(
        _ple_kernel,
        grid=(n // tm,),
        in_specs=[pl.BlockSpec((tm, d), lambda i: (i, 0)),
                  pl.BlockSpec((1, d), lambda i: (0, 0)),
                  pl.BlockSpec((tm, pd), lambda i: (i, 0)),
                  pl.BlockSpec((d, d), lambda i: (0, 0)),
                  pl.BlockSpec((pd, d), lambda i: (0, 0))],
        out_specs=pl.BlockSpec((tm, d), lambda i: (i, 0)),
        out_shape=jax.ShapeDtypeStruct((n, d), F32),
        compiler_params=_cp(("parallel",)),
        name="ple",
    )(h, g.reshape(1, d), p, wgate16, wproj16)


def _rms_kernel(h_ref, g_ref, out_ref):
    x = h_ref[...]
    var = jnp.mean(x * x, axis=-1, keepdims=True)
    out_ref[...] = x * lax.rsqrt(var + RMS_EPS) * g_ref[...]


def rms_norm(h, g, *, tm=512):
    n, d = h.shape
    return pl.pallas_call(
        _rms_kernel,
        grid=(n // tm,),
        in_specs=[pl.BlockSpec((tm, d), lambda i: (i, 0)),
                  pl.BlockSpec((1, d), lambda i: (0, 0))],
        out_specs=pl.BlockSpec((tm, d), lambda i: (i, 0)),
        out_shape=jax.ShapeDtypeStruct((n, d), F32),
        compiler_params=_cp(("parallel",)),
        name="final_norm",
    )(h, g.reshape(1, d))


def _dil_prompt_kernel(q_ref, kp_ref, kc_ref, vp_ref, vc_ref, o_ref, lse_ref):
    i = pl.program_id(2)
    q = q_ref[...]
    kk = jnp.concatenate([kp_ref[...], kc_ref[...]], axis=0)
    vv = jnp.concatenate([vp_ref[...], vc_ref[...]], axis=0)
    qu = i * Q_BLK + _iota((Q_BLK, 2 * Q_BLK), 0)
    ku = (i - 1) * Q_BLK + _iota((Q_BLK, 2 * Q_BLK), 1)
    dist = qu - ku
    valid = (ku >= 0) & (dist >= 0) & (dist <= Q_BLK)
    for h in range(DIL_HEADS):
        sl = slice(h * HEAD_DIM, (h + 1) * HEAD_DIM)
        s = jnp.where(valid, _nt(q[:, sl], kk[:, sl]), NEG_INF)
        m = jnp.max(s, axis=-1, keepdims=True)
        e = jnp.exp(s - m)
        den = jnp.sum(e, axis=-1, keepdims=True)
        o = _dot(e.astype(BF16), vv[:, sl]) / den
        o_ref[:, sl] = o
        lse_ref[:, sl] = jnp.broadcast_to(m + jnp.log(den), (Q_BLK, HEAD_DIM))


def dil_prompt(qkv16, g, *, bsz, seq):
    r = DIL_RATES[g]
    ncol = qkv16.shape[1]
    nslab = ncol // DIL_W
    view = qkv16.reshape(qkv16.shape[0] // r, r * ncol)
    n_i = seq // r // Q_BLK
    assert DIL_WINDOWS[g] // r == Q_BLK and seq % (r * Q_BLK) == 0

    def spec(slab, prev):
        def imap(b, c, i):
            ii = jnp.maximum(i - 1, 0) if prev else i
            return (b * n_i + ii, c * nslab + 3 * g + slab)
        return pl.BlockSpec((Q_BLK, DIL_W), imap)

    out_spec = pl.BlockSpec((Q_BLK, DIL_W), lambda b, c, i: (b * n_i + i, c))
    shp = jax.ShapeDtypeStruct((bsz * seq // r, r * DIL_W), F32)
    o, lse = pl.pallas_call(
        _dil_prompt_kernel,
        grid=(bsz, r, n_i),
        in_specs=[spec(0, False), spec(1, True), spec(1, False), spec(2, True), spec(2, False)],
        out_specs=[out_spec, out_spec],
        out_shape=[shp, shp],
        compiler_params=_cp(("parallel", "parallel", "arbitrary")),
        name=f"dil_prompt_g{g}",
    )(view, view, view, view, view)
    return o.reshape(bsz * seq, DIL_W), lse.reshape(bsz * seq, DIL_W)


def _dil_merge_kernel(o0, o1, o2, l0, l1, l2, out_ref):
    a, b, c = l0[...], l1[...], l2[...]
    m = jnp.maximum(jnp.maximum(a, b), c)
    wa, wb, wc = jnp.exp(a - m), jnp.exp(b - m), jnp.exp(c - m)
    out = (wa * o0[...] + wb * o1[...] + wc * o2[...]) / (wa + wb + wc)
    out_ref[...] = out.astype(BF16)


def dil_merge(os_, lses, *, tm=1024):
    n = os_[0].shape[0]
    spec = pl.BlockSpec((tm, DIL_W), lambda i: (i, 0))
    return pl.pallas_call(
        _dil_merge_kernel,
        grid=(n // tm,),
        in_specs=[spec] * 6,
        out_specs=spec,
        out_shape=jax.ShapeDtypeStruct((n, DIL_W), BF16),
        compiler_params=_cp(("parallel",)),
        name="dil_merge",
    )(*os_, *lses)


def _dil_sample_kernel(qn_ref, c0_ref, c1_ref, c2_ref, out_ref, *, n_new):
    rows = n_new * DIL_HEADS
    bd = jnp.where((_iota((DIL_HEADS, DIL_W), 1) >> 6) == _iota((DIL_HEADS, DIL_W), 0), 1.0, 0.0)
    t_row = _iota((rows, 1), 0) >> 3
    caches = (c0_ref, c1_ref, c2_ref)
    outs, lses = [], []
    for g in range(3):
        base = g * 3 * DIL_W
        qg = qn_ref[:, base:base + DIL_W] * ATTN_SCALE
        kn = qn_ref[:, base + DIL_W:base + 2 * DIL_W]
        vn = qn_ref[:, base + 2 * DIL_W:base + 3 * DIL_W]
        qh = jnp.concatenate([qg[t:t + 1, :] * bd for t in range(n_new)], axis=0)
        past = caches[g][...]
        width = past.shape[1]
        past16 = past.astype(BF16)
        n_past = past.shape[0]
        kidx = _iota((rows, n_past), 1)
        tnew = _iota((rows, n_new), 1)
        if g == 0:
            s_past = _nt(qh.astype(BF16), past16[:, :DIL_W])
            ok_past = kidx >= t_row
            ok_new = tnew <= t_row
        else:
            zeros = lambda w: jnp.zeros((DIL_HEADS, w), F32)
            blocks = []
            for t in range(n_new):
                lo = t * 2 * DIL_W
                parts = ([zeros(lo)] if lo else []) + [qh[t * DIL_HEADS:(t + 1) * DIL_HEADS]]
                hi = width - lo - DIL_W
                parts += [zeros(hi)] if hi else []
                blocks.append(jnp.concatenate(parts, axis=1))
            qfull = jnp.concatenate(blocks, axis=0)
            s_past = _nt(qfull.astype(BF16), past16)
            ok_past = kidx >= 0
            ok_new = tnew == t_row
        s_new = _nt(qh.astype(BF16), kn.astype(BF16))
        s_past = jnp.where(ok_past, s_past, NEG_INF)
        s_new = jnp.where(ok_new, s_new, NEG_INF)
        m = jnp.maximum(jnp.max(s_past, axis=-1, keepdims=True), jnp.max(s_new, axis=-1, keepdims=True))
        e_past = jnp.where(ok_past, jnp.exp(s_past - m), 0.0)
        e_new = jnp.where(ok_new, jnp.exp(s_new - m), 0.0)
        den = jnp.sum(e_past, axis=-1, keepdims=True) + jnp.sum(e_new, axis=-1, keepdims=True)
        o_full = _dot(e_past.astype(BF16), past16)
        if g == 0:
            o_sel = o_full[:, DIL_W:2 * DIL_W]
        else:
            o_sel = jnp.concatenate(
                [o_full[t * DIL_HEADS:(t + 1) * DIL_HEADS, t * 2 * DIL_W + DIL_W:(t + 1) * 2 * DIL_W]
                 for t in range(n_new)], axis=0)
        vn16 = vn.astype(BF16).astype(F32)
        e_new16 = e_new.astype(BF16).astype(F32)
        for t in range(n_new):
            o_sel = o_sel + e_new16[:, t:t + 1] * vn16[t:t + 1, :]
        outs.append(o_sel / den)
        lses.append(m + jnp.log(den))
    mm = jnp.maximum(jnp.maximum(lses[0], lses[1]), lses[2])
    ws = [jnp.exp(l - mm) for l in lses]
    tot = (ws[0] * outs[0] + ws[1] * outs[1] + ws[2] * outs[2]) / (ws[0] + ws[1] + ws[2])
    bdt = jnp.concatenate([bd] * n_new, axis=0)
    tot = tot * bdt
    out_ref[...] = jnp.concatenate(
        [jnp.sum(tot[t * DIL_HEADS:(t + 1) * DIL_HEADS], axis=0, keepdims=True) for t in range(n_new)],
        axis=0)


def dil_sample(qkv_new, c0, c1, c2, layer):
    nb, n_new, ncol = qkv_new.shape
    views = []
    for g, c in enumerate((c0, c1, c2)):
        r = DIL_RATES[g]
        w = c.shape[2]
        assert w == DIL_WINDOWS[g] and (r == 1 or r >= n_new)
        views.append(c.reshape(c.shape[0], nb, w // r, r * 2 * DIL_W))
    wid = lambda g: 2 * DIL_W if DIL_RATES[g] == 1 else n_new * 2 * DIL_W
    cspec = lambda g: pl.BlockSpec((None, None, Q_BLK, wid(g)), lambda b: (layer, b, 0, 0))
    return pl.pallas_call(
        functools.partial(_dil_sample_kernel, n_new=n_new),
        grid=(nb,),
        in_specs=[pl.BlockSpec((None, n_new, ncol), lambda b: (b, 0, 0)), cspec(0), cspec(1), cspec(2)],
        out_specs=pl.BlockSpec((None, n_new, DIL_W), lambda b: (b, 0, 0)),
        out_shape=jax.ShapeDtypeStruct((nb, n_new, DIL_W), F32),
        compiler_params=_cp(("parallel",)),
        name="dil_sample",
    )(qkv_new, *views)


def _gather_kernel(pt_ref, *refs):
    out_ref = refs[-1]
    page = refs[0].shape[0]
    for p, ref in enumerate(refs[:-1]):
        out_ref[p * page:(p + 1) * page, :] = ref[...]


def gather_pages(cache4, page_table, layer):
    nb, npg = page_table.shape
    page, w = cache4.shape[2], cache4.shape[3]

    def spec(p):
        return pl.BlockSpec((None, None, page, w), lambda b, pt: (layer, pt[b, p], 0, 0))

    grid_spec = pltpu.PrefetchScalarGridSpec(
        num_scalar_prefetch=1, grid=(nb,),
        in_specs=[spec(p) for p in range(npg)],
        out_specs=pl.BlockSpec((npg * page, w), lambda b, pt: (b, 0)))
    return pl.pallas_call(
        _gather_kernel, grid_spec=grid_spec,
        out_shape=jax.ShapeDtypeStruct((nb * npg * page, w), cache4.dtype),
        compiler_params=_cp(("parallel",)),
        name="gather_pages",
    )(page_table, *([cache4] * npg))


def _gelu_tanh(x):
    return x * (0.5 * (1.0 + jnp.tanh(np.sqrt(2.0 / np.pi) * (x + 0.044715 * (x * x * x)))))


def _compress_kernel(xk_ref, xv_ref, pe_ref, w1_ref, w2_ref, k_ref, v_ref, *, n_chunk):
    outs = (k_ref, v_ref)
    for s, x_ref in enumerate((xk_ref, xv_ref)):
        top = jnp.zeros((n_chunk, 2 * CMP_HIDDEN), F32)
        bot = jnp.zeros((n_chunk, 2 * CMP_HIDDEN), F32)
        for l in range(CMP_STRIDE):
            r = x_ref[pl.ds(l, n_chunk, stride=CMP_STRIDE), :]
            top = top + _dot((r + pe_ref[s, l:l + 1, :]).astype(BF16), w1_ref[s, l])
            lb = CMP_STRIDE + l
            bot = bot + _dot((r + pe_ref[s, lb:lb + 1, :]).astype(BF16), w1_ref[s, lb])
        pre = top + pltpu.roll(bot, n_chunk - 1, 0)
        outs[s][...] = _dot(_gelu_tanh(pre).astype(BF16), w2_ref[s])


def compress(x2d, col_blk, n_steps, pe_pair, w1bd, w2bd, *, rows=8192):
    n_chunk = rows // CMP_STRIDE
    out = jax.ShapeDtypeStruct((n_steps * n_chunk, LANES), F32)
    ospec = pl.BlockSpec((n_chunk, LANES), lambda i: (i, 0))
    return pl.pallas_call(
        functools.partial(_compress_kernel, n_chunk=n_chunk),
        grid=(n_steps,),
        in_specs=[pl.BlockSpec((rows, LANES), lambda i: (i, 2 * col_blk)),
                  pl.BlockSpec((rows, LANES), lambda i: (i, 2 * col_blk + 1)),
                  pl.BlockSpec(pe_pair.shape, lambda i: (0, 0, 0)),
                  pl.BlockSpec(w1bd.shape, lambda i: (0, 0, 0, 0)),
                  pl.BlockSpec(w2bd.shape, lambda i: (0, 0, 0))],
        out_specs=[ospec, ospec],
        out_shape=[out, out],
        compiler_params=_cp(("parallel",), VMEM_LIMIT),
        name="compress",
    )(x2d, x2d, pe_pair, w1bd, w2bd)


SEL_TILE = 512
WIN_SPAN = SLD_WINDOW + Q_BLK


def _overlap(n_cmp, n_blk):
    n = _iota((n_cmp, n_blk), 0) * CMP_STRIDE
    j = _iota((n_cmp, n_blk), 1) * SEL_LEN
    return jnp.where(n < j + SEL_LEN, jnp.where(n + CMP_LEN - 1 >= j, 1.0, 0.0), 0.0)


def _block_scores(imp, qpos):
    j = _iota(imp.shape, 1)
    cur = qpos >> 6
    eligible = j <= cur
    forced = (j == 0) | (j == cur) | (j == cur - 1)
    score = jnp.where(eligible, jnp.where(forced, FORCED_SCORE, imp), NEG_INF)
    return score, eligible


def _nsa_prompt_kernel(q_ref, kc_ref, vc_ref, ks_ref, vs_ref, kw_ref, vw_ref, gate_ref, o_ref,
                       qpad_ref, sct_ref, sel_ref, m_ref, l_ref, acc_ref, part_ref):
    i = pl.program_id(1)
    start = i * Q_BLK
    qpos = start + _iota((Q_BLK, 1), 0)
    zero = jnp.zeros((Q_BLK, HEAD_DIM), BF16)
    for h in range(NSA_HEADS):
        qh = q_ref[:, h * HEAD_DIM:(h + 1) * HEAD_DIM]
        qpad_ref[h] = jnp.concatenate([qh, zero] if h < NSA_REP else [zero, qh], axis=1)
    gate = 1.0 / (1.0 + jnp.exp(-gate_ref[...]))
    n_cmp = kc_ref.shape[0]
    kc = kc_ref[...].astype(BF16)
    vc = vc_ref[...].astype(BF16)
    ok_c = (_iota((Q_BLK, n_cmp), 1) * CMP_STRIDE + (CMP_LEN - 1)) <= qpos
    ov = _overlap(n_cmp, LANES)

    ws = pl.multiple_of(jnp.maximum(i - SLD_WINDOW // Q_BLK, 0) * Q_BLK, Q_BLK)
    kw = kw_ref[pl.ds(ws, WIN_SPAN), :]
    vw = vw_ref[pl.ds(ws, WIN_SPAN), :]
    dist = qpos - (ws + _iota((Q_BLK, WIN_SPAN), 1))
    bias_w = jnp.where(dist >= 0, jnp.where(dist <= SLD_WINDOW, 0.0, NEG_INF), NEG_INF)

    for g in range(2):
        psum = jnp.zeros((Q_BLK, n_cmp), F32)
        for r in range(NSA_REP):
            h = g * NSA_REP + r
            s = jnp.where(ok_c, _nt(qpad_ref[h], kc), NEG_INF)
            m = jnp.max(s, axis=-1, keepdims=True)
            e = jnp.where(ok_c, jnp.exp(s - m), 0.0)
            den = jnp.sum(e, axis=-1, keepdims=True)
            p = e / jnp.where(den > 0, den, 1.0)
            psum = psum + p
            part_ref[h] = gate[:, h:h + 1] * _dot(p.astype(BF16), vc)
        imp = jnp.dot(psum, ov, preferred_element_type=F32, precision=lax.Precision.HIGHEST)
        score, eligible = _block_scores(imp, qpos)
        sct = score.T
        sct_ref[...] = sct
        jrow = _iota((LANES, Q_BLK), 0)

        def rank_body(i2, cnt):
            row = sct_ref[pl.ds(i2, 1), :]
            tie = jnp.where(jrow > i2, 1.0, 0.0)
            return cnt + jnp.where(row > sct, 1.0, jnp.where(row == sct, tie, 0.0))

        cnt = lax.fori_loop(0, 2 * i + 2, rank_body, jnp.zeros((LANES, Q_BLK), F32))
        sel_t = jnp.where(cnt < SEL_TOPK, 1.0, 0.0)
        sel_ref[...] = jnp.where(eligible, sel_t.T, 0.0)

        for r in range(NSA_REP):
            m_ref[r] = jnp.full((Q_BLK, 1), NEG_INF, F32)
            l_ref[r] = jnp.zeros((Q_BLK, 1), F32)
            acc_ref[r] = jnp.zeros((Q_BLK, LANES), F32)

        def sel_body(kt, carry):
            k0 = pl.multiple_of(kt * SEL_TILE, SEL_TILE)
            ks = ks_ref[pl.ds(k0, SEL_TILE), :]
            vs = vs_ref[pl.ds(k0, SEL_TILE), :]
            blk = kt * (SEL_TILE // SEL_LEN) + (_iota((LANES, SEL_TILE), 1) >> 6)
            expand = jnp.where(_iota((LANES, SEL_TILE), 0) == blk, 1.0, 0.0).astype(BF16)
            selk = _dot(sel_ref[...].astype(BF16), expand)
            tok = k0 + _iota((Q_BLK, SEL_TILE), 1)
            bias = jnp.where(selk > 0.5, jnp.where(tok <= qpos, 0.0, NEG_INF), NEG_INF)
            for r in range(NSA_REP):
                s = _nt(qpad_ref[g * NSA_REP + r], ks) + bias
                m_old = m_ref[r]
                m_new = jnp.maximum(m_old, jnp.max(s, axis=-1, keepdims=True))
                alpha = jnp.exp(m_old - m_new)
                p = jnp.exp(s - m_new)
                l_ref[r] = alpha * l_ref[r] + jnp.sum(p, axis=-1, keepdims=True)
                acc_ref[r] = alpha * acc_ref[r] + _dot(p.astype(BF16), vs)
                m_ref[r] = m_new
            return carry

        lax.fori_loop(0, (start + Q_BLK - 1) // SEL_TILE + 1, sel_body, 0)

        for r in range(NSA_REP):
            h = g * NSA_REP + r
            o_sel = acc_ref[r] / l_ref[r]
            s = _nt(qpad_ref[h], kw) + bias_w
            m = jnp.max(s, axis=-1, keepdims=True)
            e = jnp.exp(s - m)
            o_win = _dot(e.astype(BF16), vw) / jnp.sum(e, axis=-1, keepdims=True)
            tot = (part_ref[h] + gate[:, NSA_HEADS + h:NSA_HEADS + h + 1] * o_sel
                   + gate[:, 2 * NSA_HEADS + h:2 * NSA_HEADS + h + 1] * o_win)
            o_ref[:, h * HEAD_DIM:(h + 1) * HEAD_DIM] = tot[:, g * HEAD_DIM:(g + 1) * HEAD_DIM].astype(BF16)


def nsa_prompt(proj16, proj32, kcmp, vcmp, *, bsz, seq):
    n_i = seq // Q_BLK
    n_cmp = seq // CMP_STRIDE
    assert seq >= WIN_SPAN and seq % SEL_TILE == 0 and seq // SEL_LEN <= LANES
    kvspec = lambda col: pl.BlockSpec((seq, LANES), lambda b, i: (b, col // LANES))
    cspec = pl.BlockSpec((n_cmp, LANES), lambda b, i: (b, 0))
    return pl.pallas_call(
        _nsa_prompt_kernel,
        grid=(bsz, n_i),
        in_specs=[pl.BlockSpec((Q_BLK, B_Q), lambda b, i: (b * n_i + i, 0)),
                  cspec, cspec,
                  kvspec(B_KSLC), kvspec(B_VSLC), kvspec(B_KWIN), kvspec(B_VWIN),
                  pl.BlockSpec((Q_BLK, LANES), lambda b, i: (b * n_i + i, B_GATE // LANES))],
        out_specs=pl.BlockSpec((Q_BLK, B_Q), lambda b, i: (b * n_i + i, 0)),
        out_shape=jax.ShapeDtypeStruct((bsz * seq, B_Q), BF16),
        scratch_shapes=[pltpu.VMEM((NSA_HEADS, Q_BLK, LANES), BF16),
                        pltpu.VMEM((LANES, Q_BLK), F32),
                        pltpu.VMEM((Q_BLK, LANES), F32),
                        pltpu.VMEM((NSA_REP, Q_BLK, 1), F32),
                        pltpu.VMEM((NSA_REP, Q_BLK, 1), F32),
                        pltpu.VMEM((NSA_REP, Q_BLK, LANES), F32),
                        pltpu.VMEM((NSA_HEADS, Q_BLK, LANES), F32)],
        compiler_params=_cp(("parallel", "arbitrary"), VMEM_LIMIT),
        name="nsa_prompt",
    )(proj16, kcmp, vcmp, proj16, proj16, proj16, proj16, proj32)


def _nsa_sample_kernel(qpad_ref, kc_ref, vc_ref, past_ref, new_ref, win_ref, gate_ref, o_ref,
                       *, n_new, n_past):
    rows = qpad_ref.shape[0]
    rho = _iota((rows, 1), 0)
    t_row = (rho >> 3) % n_new
    qpos = n_past + t_row
    qpad = qpad_ref[...]
    gate = 1.0 / (1.0 + jnp.exp(-gate_ref[...]))
    n_pad = new_ref.shape[0]
    tnew = _iota((rows, n_pad), 1)
    new = new_ref[...]
    new16 = new.astype(BF16)
    new_r = new16.astype(F32)

    def attend(s_past, ok_past, v_past16, s_new, ok_new, v_new_r):
        s_past = jnp.where(ok_past, s_past, NEG_INF)
        s_new = jnp.where(ok_new, s_new, NEG_INF)
        m = jnp.maximum(jnp.max(s_past, axis=-1, keepdims=True), jnp.max(s_new, axis=-1, keepdims=True))
        e_past = jnp.where(ok_past, jnp.exp(s_past - m), 0.0)
        e_new = jnp.where(ok_new, jnp.exp(s_new - m), 0.0)
        den = jnp.sum(e_past, axis=-1, keepdims=True) + jnp.sum(e_new, axis=-1, keepdims=True)
        o = _dot(e_past.astype(BF16), v_past16)
        e_new_r = e_new.astype(BF16).astype(F32)
        for t in range(n_new):
            o = o + e_new_r[:, t:t + 1] * v_new_r[t:t + 1, :]
        return o / den

    n_cmp = kc_ref.shape[0]
    ok_c = (_iota((rows, n_cmp), 1) * CMP_STRIDE + (CMP_LEN - 1)) <= qpos
    s = jnp.where(ok_c, _nt(qpad, kc_ref[...].astype(BF16)), NEG_INF)
    m = jnp.max(s, axis=-1, keepdims=True)
    e = jnp.where(ok_c, jnp.exp(s - m), 0.0)
    den = jnp.sum(e, axis=-1, keepdims=True)
    p = e / jnp.where(den > 0, den, 1.0)
    o_cmp = _dot(p.astype(BF16), vc_ref[...].astype(BF16))

    same = jnp.where((_iota((rows, rows), 0) >> 3) == (_iota((rows, rows), 1) >> 3), 1.0, 0.0)
    psum = jnp.dot(same, p, preferred_element_type=F32, precision=lax.Precision.HIGHEST)
    imp = jnp.dot(psum, _overlap(n_cmp, LANES), preferred_element_type=F32, precision=lax.Precision.HIGHEST)
    score, eligible = _block_scores(imp, qpos)
    n_sel = -(-(n_past + n_new) // SEL_LEN)
    jcol = _iota(score.shape, 1)
    cnt = jnp.zeros(score.shape, F32)
    for i2 in range(n_sel):
        col = score[:, i2:i2 + 1]
        cnt = cnt + jnp.where(col > score, 1.0, jnp.where(col == score, jnp.where(jcol > i2, 1.0, 0.0), 0.0))
    sel = jnp.where(eligible, jnp.where(cnt < SEL_TOPK, 1.0, 0.0), 0.0)

    expand = jnp.where(_iota((LANES, n_past), 0) == (_iota((LANES, n_past), 1) >> 6), 1.0, 0.0).astype(BF16)
    selk = _dot(sel.astype(BF16), expand)
    past16 = past_ref[...].astype(BF16)
    blk_new = n_past // SEL_LEN
    ok_new_sel = (sel[:, blk_new:blk_new + 1] > 0.5) & (tnew <= t_row)
    o_sel = attend(_nt(qpad, past16[:, :LANES]), selk > 0.5, past16[:, LANES:],
                   _nt(qpad, new16[:, :LANES]), ok_new_sel, new_r[:, LANES:2 * LANES])

    win16 = win_ref[...].astype(BF16)
    n_win = win16.shape[0]
    ok_w = _iota((rows, n_win), 1) >= t_row + (n_win - SLD_WINDOW)
    o_win = attend(_nt(qpad, win16[:, :LANES]), ok_w, win16[:, LANES:],
                   _nt(qpad, new16[:, 2 * LANES:3 * LANES]), tnew <= t_row, new_r[:, 3 * LANES:])

    o = gate[:, 0:1] * o_cmp + gate[:, 1:2] * o_sel + gate[:, 2:3] * o_win
    o_ref[...] = jnp.where(rho < rows // 2, o[:, :HEAD_DIM], o[:, HEAD_DIM:])


def nsa_sample(qpad, kcmp, vcmp, past2d, newkv, win_cache, gates, layer, *, n_new, n_past):
    nb, rows, _ = qpad.shape
    n_cmp = n_past // CMP_STRIDE
    n_win = win_cache.shape[2]
    assert n_past % SEL_LEN == 0 and n_new <= SEL_LEN and n_win >= SLD_WINDOW
    return pl.pallas_call(
        functools.partial(_nsa_sample_kernel, n_new=n_new, n_past=n_past),
        grid=(nb,),
        in_specs=[pl.BlockSpec((None, rows, LANES), lambda b: (b, 0, 0)),
                  pl.BlockSpec((n_cmp, LANES), lambda b: (b, 0)),
                  pl.BlockSpec((n_cmp, LANES), lambda b: (b, 0)),
                  pl.BlockSpec((n_past, 2 * LANES), lambda b: (b, 1)),
                  pl.BlockSpec((None,) + newkv.shape[1:], lambda b: (b, 0, 0)),
                  pl.BlockSpec((None, None, n_win, 2 * LANES), lambda b: (layer, b, 0, 0)),
                  pl.BlockSpec((None, rows, LANES), lambda b: (b, 0, 0))],
        out_specs=pl.BlockSpec((None, rows, HEAD_DIM), lambda b: (b, 0, 0)),
        out_shape=jax.ShapeDtypeStruct((nb, rows, HEAD_DIM), F32),
        compiler_params=_cp(("parallel",), VMEM_LIMIT),
        name="nsa_sample",
    )(qpad, kcmp, vcmp, past2d, newkv, win_cache, gates)


def _rope_tables(pos):
    inv = ROPE_THETA ** (-jnp.arange(HALF, dtype=F32) / HALF)
    ang = pos.astype(F32)[:, None] * inv[None, :]
    cos, sin, zero = jnp.cos(ang), jnp.sin(ang), jnp.zeros_like(ang)
    return (jnp.concatenate([cos, cos] * 2, axis=1),
            jnp.concatenate([-sin, zero] * 2, axis=1),
            jnp.concatenate([zero, sin] * 2, axis=1))


def _column_roles_a():
    slab = (np.arange(3 * 3 * DIL_W) % (3 * DIL_W)) // DIL_W
    rope = (slab < 2).astype(np.float32)
    scale = np.where(slab == 0, ATTN_SCALE, 1.0).astype(np.float32)
    return rope[None, :], scale[None, :]


def _column_roles_b():
    col = np.arange(B_COLS)
    is_q = col < B_Q
    is_k = (col >= B_KCMP) & (col < B_GATE) & ((col - B_KCMP) % (2 * LANES) < LANES)
    rope = (is_q | is_k).astype(np.float32)
    scale = np.where(is_q, ATTN_SCALE, 1.0).astype(np.float32)
    return rope[None, :], scale[None, :]


def _block_diag2(w):
    z = jnp.zeros_like(w)
    return jnp.concatenate([jnp.concatenate([w, z], axis=-1), jnp.concatenate([z, w], axis=-1)], axis=-2)


def kernel(x_prompt, x_sample, p_prompt, p_sample, cache_a_w128, cache_a_w512, cache_a_w2048, cache_b_kv, cache_b_win, page_table, norm_mix, norm_ffn, norm_ple, norm_final, a_w_in, a_w_out, b_w_in, b_w_out, b_cmp_pe, b_cmp_w1, b_cmp_w2, moe_w_group, moe_b_group, moe_w_expert, moe_b_expert, moe_w_gate, moe_w_up, moe_w_down, ple_w_proj, ple_w_gate):
    bsz, seq, d = x_prompt.shape
    nb, n_new, _ = x_sample.shape
    depth = norm_mix.shape[0]
    n_pages, page = page_table.shape[1], cache_b_kv.shape[2]
    n_past = n_pages * page
    np_ = bsz * seq
    ns = nb * n_new

    h = jnp.concatenate([x_prompt.reshape(np_, d), x_sample.reshape(ns, d)], axis=0)
    pos = jnp.concatenate([jnp.tile(jnp.arange(seq, dtype=jnp.int32), bsz),
                           jnp.tile(n_past + jnp.arange(n_new, dtype=jnp.int32), nb)])
    rope = _rope_tables(pos)
    roles_a = tuple(jnp.asarray(a) for a in _column_roles_a())
    roles_b = tuple(jnp.asarray(a) for a in _column_roles_b())

    caches_a = [c.reshape(c.shape[0], nb, c.shape[2], 2 * DIL_W) for c in (cache_a_w128, cache_a_w512, cache_a_w2048)]
    kv_w = cache_b_kv.shape[3] * cache_b_kv.shape[4] * cache_b_kv.shape[5]
    cache_kv4 = cache_b_kv.reshape(cache_b_kv.shape[0], cache_b_kv.shape[1], page, kv_w)
    cache_win4 = cache_b_win.reshape(cache_b_win.shape[0], nb, cache_b_win.shape[2], 2 * LANES)

    a_bufs_p, a_bufs_s, b_rows_p, b_rows_s, b_win_p, b_win_s = [], [], [], [], [], []
    for i in range(depth):
        li = i // 2
        if i % 2 == 0:
            qkv32, qkv16 = norm_proj(h, norm_mix[i], a_w_in[li].astype(BF16), *roles_a, *rope)
            parts = [dil_prompt(qkv16, g, bsz=bsz, seq=seq) for g in range(3)]
            o_p = dil_merge([p[0] for p in parts], [p[1] for p in parts])
            qkv_new = qkv32[np_:].reshape(nb, n_new, qkv32.shape[1])
            o_s = dil_sample(qkv_new, *caches_a, li)
            o16 = jnp.concatenate([o_p, o_s.reshape(ns, DIL_W).astype(BF16)], axis=0)
            h = out_proj(h, o16, a_w_out[li].astype(BF16))
            qkv_p = qkv32[:np_].reshape(bsz, seq, qkv32.shape[1])
            bp, bs = [], []
            for g, cache in enumerate((cache_a_w128, cache_a_w512, cache_a_w2048)):
                lo, hi = g * 3 * DIL_W + DIL_W, (g + 1) * 3 * DIL_W
                keep = min(DIL_WINDOWS[g], seq)
                bp.append(qkv_p[:, seq - keep:, lo:hi].reshape(bsz, keep, 2, DIL_HEADS, HEAD_DIM))
                new = qkv_new[:, :, lo:hi].reshape(nb, n_new, 2, DIL_HEADS, HEAD_DIM)
                bs.append(jnp.concatenate([cache[li][:, n_new:], new], axis=1))
            a_bufs_p.append(bp)
            a_bufs_s.append(bs)
        else:
            w_in = jnp.pad(b_w_in[li], ((0, 0), (0, B_COLS - b_w_in.shape[2]))).astype(BF16)
            p32, p16 = norm_proj(h, norm_mix[i], w_in, *roles_b, *rope)
            pe_pair = jnp.concatenate([b_cmp_pe[li], b_cmp_pe[li]], axis=-1)
            w1bd = _block_diag2(b_cmp_w1[li].reshape(2, CMP_LEN, HEAD_DIM, CMP_HIDDEN)).astype(BF16)
            w2bd = _block_diag2(b_cmp_w2[li]).astype(BF16)
            kc_p, vc_p = compress(p32, B_KCMP // (2 * LANES), bsz, pe_pair, w1bd, w2bd, rows=seq)
            o_p = nsa_prompt(p16, p32, kc_p, vc_p, bsz=bsz, seq=seq)
            past = gather_pages(cache_kv4, page_table, li)
            step_rows = max(n_past, 8192)
            kc_s, vc_s = compress(past, 0, nb * n_past // step_rows, pe_pair, w1bd, w2bd, rows=step_rows)
            ps32, ps16 = p32[np_:], p16[np_:]
            q = ps16[:, :B_Q].reshape(nb, n_new, 2, NSA_REP, HEAD_DIM).transpose(0, 2, 1, 3, 4)
            zq = jnp.zeros_like(q[:, 0])
            qpad = jnp.stack([jnp.concatenate([q[:, 0], zq], axis=-1),
                              jnp.concatenate([zq, q[:, 1]], axis=-1)], axis=1)
            qpad = qpad.reshape(nb, 2 * n_new * NSA_REP, LANES)
            gts = ps32[:, B_GATE:B_GATE + 3 * NSA_HEADS].reshape(nb, n_new, 3, 2, NSA_REP)
            gts = gts.transpose(0, 3, 1, 4, 2).reshape(nb, 2 * n_new * NSA_REP, 3)
            gts = jnp.pad(gts, ((0, 0), (0, 0), (0, LANES - 3)))
            newkv = ps32[:, B_KSLC:B_GATE].reshape(nb, n_new, B_GATE - B_KSLC)
            newkv = jnp.pad(newkv, ((0, 0), (0, 8 - n_new), (0, 0)))
            o_s = nsa_sample(qpad, kc_s, vc_s, past, newkv, cache_win4, gts, li, n_new=n_new, n_past=n_past)
            o_s = o_s.reshape(nb, 2, n_new, NSA_REP, HEAD_DIM).transpose(0, 2, 1, 3, 4).reshape(ns, B_Q)
            o16 = jnp.concatenate([o_p, o_s.astype(BF16)], axis=0)
            h = out_proj(h, o16, b_w_out[li].astype(BF16))
            pp = p32[:np_].reshape(bsz, seq, B_COLS)
            keep = min(SLD_WINDOW, seq)
            b_rows_p.append(pp[:, :, B_KCMP:B_KWIN].reshape(bsz, seq, 4, 2, HEAD_DIM))
            b_win_p.append(pp[:, seq - keep:, B_KWIN:B_GATE].reshape(bsz, keep, 2, 2, HEAD_DIM))
            b_rows_s.append(ps32[:, B_KCMP:B_KWIN].reshape(nb, n_new, 4, 2, HEAD_DIM))
            new_win = ps32[:, B_KWIN:B_GATE].reshape(nb, n_new, 2, 2, HEAD_DIM)
            b_win_s.append(jnp.concatenate([cache_b_win[li][:, n_new:], new_win], axis=1))

        w_route = jnp.pad(jnp.concatenate([moe_w_group[i], moe_w_expert[i]], axis=1),
                          ((0, 0), (0, LANES - N_GROUPS - N_EXPERTS)))
        b_route = jnp.pad(jnp.concatenate([moe_b_group[i], moe_b_expert[i]]),
                          (0, LANES - N_GROUPS - N_EXPERTS)).reshape(1, LANES)
        h = moe(h, norm_ffn[i], w_route, b_route, moe_w_gate[i].astype(BF16),
                moe_w_up[i].astype(BF16), moe_w_down[i].astype(BF16))
        p_tok = jnp.concatenate([p_prompt[i].reshape(np_, -1), p_sample[i].reshape(ns, -1)], axis=0)
        h = ple(h, norm_ple[i], p_tok, ple_w_gate[i].astype(BF16), ple_w_proj[i].astype(BF16))

    y = rms_norm(h, norm_final)
    stack = lambda bufs, g: jnp.stack([b[g] for b in bufs], axis=0)
    return (y[:np_].reshape(bsz, seq, d), y[np_:].reshape(nb, n_new, d),
            stack(a_bufs_p, 0), stack(a_bufs_p, 1), stack(a_bufs_p, 2),
            jnp.stack(b_rows_p, axis=0), jnp.stack(b_win_p, axis=0),
            stack(a_bufs_s, 0), stack(a_bufs_s, 1), stack(a_bufs_s, 2),
            jnp.stack(b_rows_s, axis=0), jnp.stack(b_win_s, axis=0))
```

```python
import functools

import numpy as np
import jax
import jax.numpy as jnp
from jax import lax
from jax.experimental import pallas as pl
from jax.experimental.pallas import tpu as pltpu

F32 = jnp.float32
BF16 = jnp.bfloat16

HEAD_DIM = 64
HALF = HEAD_DIM // 2
ROPE_THETA = 10000.0
RMS_EPS = 1e-6
NEG_INF = -1e30
ATTN_SCALE = HEAD_DIM ** -0.5
LANES = 128
Q_BLK = 128

DIL_WINDOWS = (128, 512, 2048)
DIL_RATES = (1, 4, 16)
DIL_HEADS = 8
DIL_W = DIL_HEADS * HEAD_DIM

NSA_HEADS = 16
NSA_REP = 8
CMP_LEN = 32
CMP_STRIDE = 16
CMP_HIDDEN = 256
SEL_LEN = 64
SEL_TOPK = 16
SLD_WINDOW = 512
FORCED_SCORE = 1e9
B_COLS = 2048
B_Q = 1024
B_KCMP, B_VCMP, B_KSLC, B_VSLC, B_KWIN, B_VWIN, B_GATE = (1024, 1152, 1280, 1408, 1536, 1664, 1792)

N_GROUPS = 4
EXP_PER_GROUP = 8
N_EXPERTS = 32

VMEM_LIMIT = 56 * 1024 * 1024


def _cp(sem, vmem=None):
    return pltpu.CompilerParams(dimension_semantics=sem, vmem_limit_bytes=vmem)


def _nt(a, b):
    return lax.dot_general(a, b, (((1,), (1,)), ((), ())), preferred_element_type=F32)


def _dot(a, b):
    return jnp.dot(a, b, preferred_element_type=F32)


def _iota(shape, dim):
    return lax.broadcasted_iota(jnp.int32, shape, dim)


def _norm_proj_kernel(x_ref, g_ref, w_ref, rc_ref, sc_ref, c_ref, s1_ref, s2_ref,
                      o32_ref, o16_ref, xn_ref, *, tn):
    @pl.when(pl.program_id(1) == 0)
    def _():
        x = x_ref[...]
        var = jnp.mean(x * x, axis=-1, keepdims=True)
        xn_ref[...] = (x * lax.rsqrt(var + RMS_EPS) * g_ref[...]).astype(BF16)

    y = _dot(xn_ref[...], w_ref[...])
    reps = tn // LANES
    rc = rc_ref[...]
    c = 1.0 + rc * (jnp.tile(c_ref[...], (1, reps)) - 1.0)
    s1 = rc * jnp.tile(s1_ref[...], (1, reps))
    s2 = rc * jnp.tile(s2_ref[...], (1, reps))
    y = y * c + pltpu.roll(y, tn - HALF, 1) * s1 + pltpu.roll(y, HALF, 1) * s2
    o32_ref[...] = y
    o16_ref[...] = (y * sc_ref[...]).astype(BF16)


def norm_proj(x, g, w16, ropecol, colscale, rope_c, rope_s1, rope_s2, *, tm=512, tn=512):
    n, d = x.shape
    ncols = w16.shape[1]
    assert n % tm == 0 and ncols % tn == 0
    return pl.pallas_call(
        functools.partial(_norm_proj_kernel, tn=tn),
        grid=(n // tm, ncols // tn),
        in_specs=[
            pl.BlockSpec((tm, d), lambda i, j: (i, 0)),
            pl.BlockSpec((1, d), lambda i, j: (0, 0)),
            pl.BlockSpec((d, tn), lambda i, j: (0, j)),
            pl.BlockSpec((1, tn), lambda i, j: (0, j)),
            pl.BlockSpec((1, tn), lambda i, j: (0, j)),
            pl.BlockSpec((tm, LANES), lambda i, j: (i, 0)),
            pl.BlockSpec((tm, LANES), lambda i, j: (i, 0)),
            pl.BlockSpec((tm, LANES), lambda i, j: (i, 0)),
        ],
        out_specs=[pl.BlockSpec((tm, tn), lambda i, j: (i, j)),
                   pl.BlockSpec((tm, tn), lambda i, j: (i, j))],
        out_shape=[jax.ShapeDtypeStruct((n, ncols), F32),
                   jax.ShapeDtypeStruct((n, ncols), BF16)],
        scratch_shapes=[pltpu.VMEM((tm, d), BF16)],
        compiler_params=_cp(("parallel", "arbitrary")),
        name="norm_proj",
    )(x, g.reshape(1, d), w16, ropecol, colscale, rope_c, rope_s1, rope_s2)


def _out_proj_kernel(h_ref, o_ref, w_ref, out_ref):
    out_ref[...] = h_ref[...] + _dot(o_ref[...], w_ref[...])


def out_proj(h, o16, w16, *, tm=512):
    n, d = h.shape
    k = o16.shape[1]
    return pl.pallas_call(
        _out_proj_kernel,
        grid=(n // tm,),
        in_specs=[pl.BlockSpec((tm, d), lambda i: (i, 0)),
                  pl.BlockSpec((tm, k), lambda i: (i, 0)),
                  pl.BlockSpec((k, d), lambda i: (0, 0))],
        out_specs=pl.BlockSpec((tm, d), lambda i: (i, 0)),
        out_shape=jax.ShapeDtypeStruct((n, d), F32),
        compiler_params=_cp(("parallel",)),
        name="out_proj",
    )(h, o16, w16)


def _moe_kernel(h_ref, g_ref, wr_ref, br_ref, wg_ref, wu_ref, wd_ref, out_ref,
                xn_ref, comb_ref, acc_ref):
    grp = pl.program_id(1)

    @pl.when(grp == 0)
    def _():
        x = h_ref[...]
        var = jnp.mean(x * x, axis=-1, keepdims=True)
        xn = x * lax.rsqrt(var + RMS_EPS) * g_ref[...]
        xn_ref[...] = xn.astype(BF16)
        logit = jnp.dot(xn, wr_ref[...], preferred_element_type=F32,
                        precision=lax.Precision.HIGHEST) + br_ref[...]
        lane = _iota(logit.shape, 1)
        is_grp = lane < N_GROUPS
        lg = jnp.where(is_grp, logit, -jnp.inf)
        gmax = jnp.max(lg, axis=-1, keepdims=True)
        gidx = jnp.min(jnp.where(lg == gmax, lane, LANES), axis=-1, keepdims=True)
        p_grp = 1.0 / jnp.sum(jnp.where(is_grp, jnp.exp(logit - gmax), 0.0), axis=-1, keepdims=True)
        lo = N_GROUPS + gidx * EXP_PER_GROUP
        in_grp = (lane >= lo) & (lane < lo + EXP_PER_GROUP)
        le = jnp.where(in_grp, logit, -jnp.inf)
        v1 = jnp.max(le, axis=-1, keepdims=True)
        i1 = jnp.min(jnp.where(le == v1, lane, LANES), axis=-1, keepdims=True)
        le2 = jnp.where(lane == i1, -jnp.inf, le)
        v2 = jnp.max(le2, axis=-1, keepdims=True)
        i2 = jnp.min(jnp.where(le2 == v2, lane, LANES), axis=-1, keepdims=True)
        e2 = jnp.exp(v2 - v1)
        w1 = p_grp / (1.0 + e2)
        w2 = p_grp * e2 / (1.0 + e2)
        comb = jnp.where(lane == i1, w1, jnp.where(lane == i2, w2, 0.0))
        for gg in range(N_GROUPS):
            sel = jnp.where(gidx == gg, comb, 0.0)
            comb_ref[gg] = pltpu.roll(sel, LANES - (N_GROUPS + gg * EXP_PER_GROUP), 1)
        acc_ref[...] = jnp.zeros_like(acc_ref)

    xn = xn_ref[...]
    comb = comb_ref[grp]
    acc = acc_ref[...]
    for k in range(EXP_PER_GROUP):
        hg = _dot(xn, wg_ref[k])
        hu = _dot(xn, wu_ref[k])
        a = (hg / (1.0 + jnp.exp(-hg))) * hu * comb[:, k:k + 1]
        acc = acc + _dot(a.astype(BF16), wd_ref[k])
    acc_ref[...] = acc

    @pl.when(grp == N_GROUPS - 1)
    def _():
        out_ref[...] = h_ref[...] + acc_ref[...]


def moe(h, g, w_route, b_route, wg16, wu16, wd16, *, tm=512):
    n, d = h.shape
    f = wg16.shape[2]
    return pl.pallas_call(
        _moe_kernel,
        grid=(n // tm, N_GROUPS),
        in_specs=[
            pl.BlockSpec((tm, d), lambda i, e: (i, 0)),
            pl.BlockSpec((1, d), lambda i, e: (0, 0)),
            pl.BlockSpec((d, LANES), lambda i, e: (0, 0)),
            pl.BlockSpec((1, LANES), lambda i, e: (0, 0)),
            pl.BlockSpec((EXP_PER_GROUP, d, f), lambda i, e: (e, 0, 0)),
            pl.BlockSpec((EXP_PER_GROUP, d, f), lambda i, e: (e, 0, 0)),
            pl.BlockSpec((EXP_PER_GROUP, f, d), lambda i, e: (e, 0, 0)),
        ],
        out_specs=pl.BlockSpec((tm, d), lambda i, e: (i, 0)),
        out_shape=jax.ShapeDtypeStruct((n, d), F32),
        scratch_shapes=[pltpu.VMEM((tm, d), BF16),
                        pltpu.VMEM((N_GROUPS, tm, LANES), F32),
                        pltpu.VMEM((tm, d), F32)],
        compiler_params=_cp(("parallel", "arbitrary"), VMEM_LIMIT),
        name="moe",
    )(h, g.reshape(1, d), w_route, b_route, wg16, wu16, wd16)


def _ple_kernel(h_ref, g_ref, p_ref, wgate_ref, wproj_ref, out_ref):
    x = h_ref[...]
    var = jnp.mean(x * x, axis=-1, keepdims=True)
    xn = (x * lax.rsqrt(var + RMS_EPS) * g_ref[...]).astype(BF16)
    gate = 1.0 / (1.0 + jnp.exp(-_dot(xn, wgate_ref[...])))
    out_ref[...] = x + gate * _dot(p_ref[...].astype(BF16), wproj_ref[...])


def ple(h, g, p, wgate16, wproj16, *, tm=512):
    n, d = h.shape
    pd = p.shape[1]
    return pl.pallas_call(
        _ple_kernel,
        grid=(n // tm,),
        in_specs=[pl.BlockSpec((tm, d), lambda i: (i, 0)),
                  pl.BlockSpec((1, d), lambda i: (0, 0)),
                  pl.BlockSpec((tm, pd), lambda i: (i, 0)),
                  pl.BlockSpec((d, d), lambda i: (0, 0)),
                  pl.BlockSpec((pd, d), lambda i: (0, 0))],
        out_specs=pl.BlockSpec((tm, d), lambda i: (i, 0)),
        out_shape=jax.ShapeDtypeStruct((n, d), F32),
        compiler_params=_cp(("parallel",)),
        name="ple",
    )(h, g.reshape(1, d), p, wgate16, wproj16)


def _rms_kernel(h_ref, g_ref, out_ref):
    x = h_ref[...]
    var = jnp.mean(x * x, axis=-1, keepdims=True)
    out_ref[...] = x * lax.rsqrt(var + RMS_EPS) * g_ref[...]


def rms_norm(h, g, *, tm=512):
    n, d = h.shape
    return pl.pallas_call(
        _rms_kernel,
        grid=(n // tm,),
        in_specs=[pl.BlockSpec((tm, d), lambda i: (i, 0)),
                  pl.BlockSpec((1, d), lambda i: (0, 0))],
        out_specs=pl.BlockSpec((tm, d), lambda i: (i, 0)),
        out_shape=jax.ShapeDtypeStruct((n, d), F32),
        compiler_params=_cp(("parallel",)),
        name="final_norm",
    )(h, g.reshape(1, d))


def _dil_prompt_kernel(q_ref, kp_ref, kc_ref, vp_ref, vc_ref, o_ref, lse_ref):
    i = pl.program_id(2)
    q = q_ref[...]
    kk = jnp.concatenate([kp_ref[...], kc_ref[...]], axis=0)
    vv = jnp.concatenate([vp_ref[...], vc_ref[...]], axis=0)
    qu = i * Q_BLK + _iota((Q_BLK, 2 * Q_BLK), 0)
    ku = (i - 1) * Q_BLK + _iota((Q_BLK, 2 * Q_BLK), 1)
    dist = qu - ku
    valid = (ku >= 0) & (dist >= 0) & (dist <= Q_BLK)
    for h in range(DIL_HEADS):
        sl = slice(h * HEAD_DIM, (h + 1) * HEAD_DIM)
        s = jnp.where(valid, _nt(q[:, sl], kk[:, sl]), NEG_INF)
        m = jnp.max(s, axis=-1, keepdims=True)
        e = jnp.exp(s - m)
        den = jnp.sum(e, axis=-1, keepdims=True)
        o = _dot(e.astype(BF16), vv[:, sl]) / den
        o_ref[:, sl] = o
        lse_ref[:, sl] = jnp.broadcast_to(m + jnp.log(den), (Q_BLK, HEAD_DIM))


def dil_prompt(qkv16, g, *, bsz, seq):
    r = DIL_RATES[g]
    ncol = qkv16.shape[1]
    nslab = ncol // DIL_W
    view = qkv16.reshape(qkv16.shape[0] // r, r * ncol)
    n_i = seq // r // Q_BLK
    assert DIL_WINDOWS[g] // r == Q_BLK and seq % (r * Q_BLK) == 0

    def spec(slab, prev):
        def imap(b, c, i):
            ii = jnp.maximum(i - 1, 0) if prev else i
            return (b * n_i + ii, c * nslab + 3 * g + slab)
        return pl.BlockSpec((Q_BLK, DIL_W), imap)

    out_spec = pl.BlockSpec((Q_BLK, DIL_W), lambda b, c, i: (b * n_i + i, c))
    shp = jax.ShapeDtypeStruct((bsz * seq // r, r * DIL_W), F32)
    o, lse = pl.pallas_call(
        _dil_prompt_kernel,
        grid=(bsz, r, n_i),
        in_specs=[spec(0, False), spec(1, True), spec(1, False), spec(2, True), spec(2, False)],
        out_specs=[out_spec, out_spec],
        out_shape=[shp, shp],
        compiler_params=_cp(("parallel", "parallel", "arbitrary")),
        name=f"dil_prompt_g{g}",
    )(view, view, view, view, view)
    return o.reshape(bsz * seq, DIL_W), lse.reshape(bsz * seq, DIL_W)


def _dil_merge_kernel(o0, o1, o2, l0, l1, l2, out_ref):
    a, b, c = l0[...], l1[...], l2[...]
    m = jnp.maximum(jnp.maximum(a, b), c)
    wa, wb, wc = jnp.exp(a - m), jnp.exp(b - m), jnp.exp(c - m)
    out = (wa * o0[...] + wb * o1[...] + wc * o2[...]) / (wa + wb + wc)
    out_ref[...] = out.astype(BF16)


def dil_merge(os_, lses, *, tm=1024):
    n = os_[0].shape[0]
    spec = pl.BlockSpec((tm, DIL_W), lambda i: (i, 0))
    return pl.pallas_call(
        _dil_merge_kernel,
        grid=(n // tm,),
        in_specs=[spec] * 6,
        out_specs=spec,
        out_shape=jax.ShapeDtypeStruct((n, DIL_W), BF16),
        compiler_params=_cp(("parallel",)),
        name="dil_merge",
    )(*os_, *lses)


NEW_PAD = 8


def _dil_sample_kernel(*refs, n_new, n_alias):
    qn_ref, c0, c1, c2 = refs[:4]
    o_ref, u0, u1, u2, e0, e1, e2, en_ref, st_ref = refs[4 + n_alias:]
    caches, ups, es = (c0, c1, c2), (u0, u1, u2), (e0, e1, e2)
    kv = pl.program_id(1)
    rows = n_new * DIL_HEADS
    off = NEW_PAD - n_new
    bd = jnp.where((_iota((DIL_HEADS, DIL_W), 1) >> 6) == _iota((DIL_HEADS, DIL_W), 0), 1.0, 0.0)
    t_row = _iota((rows, 1), 0) >> 3
    lane = _iota((DIL_W, LANES), 1)

    def shifted(x, new8):
        w = x.shape[1]
        new_t = jnp.concatenate([jnp.zeros((LANES - NEW_PAD, DIL_W), F32), new8], axis=0).T
        rolled = pltpu.roll(x, w - n_new, 1)
        tail = jnp.where(lane >= LANES - n_new, new_t, rolled[:, w - LANES:])
        return tail if w == LANES else jnp.concatenate([rolled[:, :w - LANES], tail], axis=1)

    @pl.when(kv == 0)
    def _():
        for g in range(3):
            rate = DIL_RATES[g]
            base = g * 3 * DIL_W
            qg = qn_ref[:, base:base + DIL_W] * ATTN_SCALE
            kn = qn_ref[:, base + DIL_W:base + 2 * DIL_W]
            qh = jnp.concatenate([qg[off + t:off + t + 1, :] * bd for t in range(n_new)], axis=0).astype(BF16)
            x = caches[g][...]
            w = x.shape[1]
            s_past = _dot(qh, x.astype(BF16))
            gap = _iota((rows, w), 1) - t_row
            ok_past = (gap >= 0) & ((gap & (rate - 1)) == 0)
            s_new = _nt(qh, kn.astype(BF16))
            gap_n = t_row - (_iota((rows, NEW_PAD), 1) - off)
            ok_new = (gap_n >= 0) & (gap_n <= t_row) & ((gap_n & (rate - 1)) == 0)
            s_past = jnp.where(ok_past, s_past, NEG_INF)
            s_new = jnp.where(ok_new, s_new, NEG_INF)
            m = jnp.maximum(jnp.max(s_past, axis=-1, keepdims=True), jnp.max(s_new, axis=-1, keepdims=True))
            e_past = jnp.where(ok_past, jnp.exp(s_past - m), 0.0)
            e_new = jnp.where(ok_new, jnp.exp(s_new - m), 0.0)
            den = jnp.sum(e_past, axis=-1, keepdims=True) + jnp.sum(e_new, axis=-1, keepdims=True)
            es[g][...] = e_past.astype(BF16)
            en_ref[g] = e_new
            st_ref[g, 0] = m + jnp.log(den)
            st_ref[g, 1] = 1.0 / den
            ups[g][...] = shifted(x, kn)

    @pl.when(kv == 1)
    def _():
        outs, lses = [], []
        for g in range(3):
            base = g * 3 * DIL_W
            vn = qn_ref[:, base + 2 * DIL_W:base + 3 * DIL_W]
            x = caches[g][...]
            o = _nt(es[g][...], x.astype(BF16))
            e_new = en_ref[g].astype(BF16).astype(F32)
            vn_r = vn.astype(BF16).astype(F32)
            for t in range(n_new):
                o = o + e_new[:, off + t:off + t + 1] * vn_r[off + t:off + t + 1, :]
            outs.append(o * st_ref[g, 1])
            lses.append(st_ref[g, 0])
            ups[g][...] = shifted(x, vn)
        mm = jnp.maximum(jnp.maximum(lses[0], lses[1]), lses[2])
        ws = [jnp.exp(l - mm) for l in lses]
        tot = (ws[0] * outs[0] + ws[1] * outs[1] + ws[2] * outs[2]) / (ws[0] + ws[1] + ws[2])
        tot = tot * jnp.concatenate([bd] * n_new, axis=0)
        o_ref[...] = jnp.concatenate(
            [jnp.sum(tot[t * DIL_HEADS:(t + 1) * DIL_HEADS], axis=0, keepdims=True) for t in range(n_new)], axis=0)


def dil_sample(qkv_new8, caches_t, prev, layer, *, n_new):
    nb, _, ncol = qkv_new8.shape
    for g, c in enumerate(caches_t):
        assert c.shape[4] == DIL_WINDOWS[g] and DIL_RATES[g] & (DIL_RATES[g] - 1) == 0
    rows = n_new * DIL_HEADS
    cspec = lambda c: pl.BlockSpec((None, None, None, DIL_W, c.shape[4]), lambda b, kv: (layer, b, kv, 0, 0))
    n_alias = 0 if prev is None else 3
    any_spec = pl.BlockSpec(memory_space=pl.ANY)
    outs = pl.pallas_call(
        functools.partial(_dil_sample_kernel, n_new=n_new, n_alias=n_alias),
        grid=(nb, 2),
        in_specs=[pl.BlockSpec((None, NEW_PAD, ncol), lambda b, kv: (b, 0, 0))]
                 + [cspec(c) for c in caches_t] + [any_spec] * n_alias,
        out_specs=[pl.BlockSpec((None, n_new, DIL_W), lambda b, kv: (b, 0, 0))] + [cspec(c) for c in caches_t],
        out_shape=[jax.ShapeDtypeStruct((nb, n_new, DIL_W), F32)]
                  + [jax.ShapeDtypeStruct(c.shape, F32) for c in caches_t],
        scratch_shapes=[pltpu.VMEM((rows, c.shape[4]), BF16) for c in caches_t]
                       + [pltpu.VMEM((3, rows, NEW_PAD), F32), pltpu.VMEM((3, 2, rows, 1), F32)],
        input_output_aliases={} if prev is None else {4 + g: 1 + g for g in range(3)},
        compiler_params=_cp(("parallel", "arbitrary"), VMEM_LIMIT),
        name="dil_sample",
    )(qkv_new8, *caches_t, *(() if prev is None else prev))
    return outs[0], tuple(outs[1:])


def _gather_kernel(pt_ref, *refs):
    out_ref = refs[-1]
    page = refs[0].shape[0]
    for p, ref in enumerate(refs[:-1]):
        out_ref[p * page:(p + 1) * page, :] = ref[...]


def gather_pages(cache4, page_table, layer):
    nb, npg = page_table.shape
    page, w = cache4.shape[2], cache4.shape[3]

    def spec(p):
        return pl.BlockSpec((None, None, page, w), lambda b, pt: (layer, pt[b, p], 0, 0))

    grid_spec = pltpu.PrefetchScalarGridSpec(
        num_scalar_prefetch=1, grid=(nb,),
        in_specs=[spec(p) for p in range(npg)],
        out_specs=pl.BlockSpec((npg * page, w), lambda b, pt: (b, 0)))
    return pl.pallas_call(
        _gather_kernel, grid_spec=grid_spec,
        out_shape=jax.ShapeDtypeStruct((nb * npg * page, w), cache4.dtype),
        compiler_params=_cp(("parallel",)),
        name="gather_pages",
    )(page_table, *([cache4] * npg))


def _gelu_tanh(x):
    return x * (0.5 * (1.0 + jnp.tanh(np.sqrt(2.0 / np.pi) * (x + 0.044715 * (x * x * x)))))


def _compress_kernel(xk_ref, xv_ref, pe_ref, w1_ref, w2_ref, k_ref, v_ref, *, n_chunk):
    outs = (k_ref, v_ref)
    for s, x_ref in enumerate((xk_ref, xv_ref)):
        top = jnp.zeros((n_chunk, 2 * CMP_HIDDEN), F32)
        bot = jnp.zeros((n_chunk, 2 * CMP_HIDDEN), F32)
        for l in range(CMP_STRIDE):
            r = x_ref[pl.ds(l, n_chunk, stride=CMP_STRIDE), :]
            top = top + _dot((r + pe_ref[s, l:l + 1, :]).astype(BF16), w1_ref[s, l])
            lb = CMP_STRIDE + l
            bot = bot + _dot((r + pe_ref[s, lb:lb + 1, :]).astype(BF16), w1_ref[s, lb])
        pre = top + pltpu.roll(bot, n_chunk - 1, 0)
        outs[s][...] = _dot(_gelu_tanh(pre).astype(BF16), w2_ref[s])


def compress(x2d, col_blk, n_steps, pe_pair, w1bd, w2bd, *, rows=8192):
    n_chunk = rows // CMP_STRIDE
    out = jax.ShapeDtypeStruct((n_steps * n_chunk, LANES), F32)
    ospec = pl.BlockSpec((n_chunk, LANES), lambda i: (i, 0))
    return pl.pallas_call(
        functools.partial(_compress_kernel, n_chunk=n_chunk),
        grid=(n_steps,),
        in_specs=[pl.BlockSpec((rows, LANES), lambda i: (i, 2 * col_blk)),
                  pl.BlockSpec((rows, LANES), lambda i: (i, 2 * col_blk + 1)),
                  pl.BlockSpec(pe_pair.shape, lambda i: (0, 0, 0)),
                  pl.BlockSpec(w1bd.shape, lambda i: (0, 0, 0, 0)),
                  pl.BlockSpec(w2bd.shape, lambda i: (0, 0, 0))],
        out_specs=[ospec, ospec],
        out_shape=[out, out],
        compiler_params=_cp(("parallel",), VMEM_LIMIT),
        name="compress",
    )(x2d, x2d, pe_pair, w1bd, w2bd)


SEL_TILE = 512
WIN_SPAN = SLD_WINDOW + Q_BLK


def _overlap(n_cmp, n_blk):
    n = _iota((n_cmp, n_blk), 0) * CMP_STRIDE
    j = _iota((n_cmp, n_blk), 1) * SEL_LEN
    return jnp.where(n < j + SEL_LEN, jnp.where(n + CMP_LEN - 1 >= j, 1.0, 0.0), 0.0)


def _block_scores(imp, qpos):
    j = _iota(imp.shape, 1)
    cur = qpos >> 6
    eligible = j <= cur
    forced = (j == 0) | (j == cur) | (j == cur - 1)
    score = jnp.where(eligible, jnp.where(forced, FORCED_SCORE, imp), NEG_INF)
    return score, eligible


def _overlap_t(n_cmp):
    j = _iota((LANES, n_cmp), 0) * SEL_LEN
    n = _iota((LANES, n_cmp), 1) * CMP_STRIDE
    return jnp.where(n < j + SEL_LEN, jnp.where(n + CMP_LEN - 1 >= j, 1.0, 0.0), 0.0)


def _rank_select(sct, eligible_t):
    n_t = LANES // 8
    tiles = [sct[8 * v:8 * v + 8, :] for v in range(n_t)]
    cnts = [jnp.zeros((8, Q_BLK), F32) for _ in range(n_t)]
    sub = _iota((8, Q_BLK), 0)
    for i2 in range(LANES):
        row = jnp.broadcast_to(sct[i2:i2 + 1, :], (8, Q_BLK))
        u = i2 // 8
        for v in range(n_t):
            if u < v:
                ahead = row >= tiles[v]
            elif u > v:
                ahead = row > tiles[v]
            else:
                ahead = (row > tiles[v]) | ((row == tiles[v]) & (sub > i2 % 8))
            cnts[v] = cnts[v] + jnp.where(ahead, 1.0, 0.0)
    cnt = jnp.concatenate(cnts, axis=0)
    return jnp.where(eligible_t, jnp.where(cnt < SEL_TOPK, 1.0, 0.0), 0.0)


def _row_chunks(n):
    return [(a, a + Q_BLK) for a in range(0, n, Q_BLK)]


def _nsa_prompt_kernel(q_ref, kc_ref, vct_ref, ks_ref, vst_ref, kw_ref, vwt_ref, gate_ref, o_ref,
                       qst_ref, sc_ref, p_ref, ps_ref, m_ref, l_ref, al_ref, acc_ref, part_ref):
    i = pl.program_id(1)
    start = i * Q_BLK
    qpos = start + _iota((1, Q_BLK), 1)
    gate_t = (1.0 / (1.0 + jnp.exp(-gate_ref[...]))).T
    n_cmp = kc_ref.shape[0]
    kc = kc_ref[...].astype(BF16)
    vct = vct_ref[...].astype(BF16)
    ov_t = _overlap_t(n_cmp)
    blk_t = _iota((LANES, Q_BLK), 0)
    cur = qpos >> 6
    eligible_t = blk_t <= cur
    forced_t = (blk_t == 0) | (blk_t == cur) | (blk_t == cur - 1)
    ws = pl.multiple_of(jnp.maximum(i - SLD_WINDOW // Q_BLK, 0) * Q_BLK, Q_BLK)
    sub_iota = _iota((Q_BLK, Q_BLK), 0)
    zero = jnp.zeros((Q_BLK, HEAD_DIM), BF16)
    cols = [slice(r * Q_BLK, (r + 1) * Q_BLK) for r in range(NSA_REP)]
    neg_row = jnp.full((1, Q_BLK), NEG_INF, F32)
    n_kt = (start + Q_BLK - 1) // SEL_TILE + 1

    for g in range(2):
        for r in range(NSA_REP):
            h = g * NSA_REP + r
            qh = q_ref[:, h * HEAD_DIM:(h + 1) * HEAD_DIM]
            qst_ref[r * Q_BLK:(r + 1) * Q_BLK, 0:LANES] = jnp.concatenate([qh, zero] if g == 0 else [zero, qh], axis=1)

        sc_ref[0:n_cmp, :] = _nt(kc, qst_ref[:, 0:LANES])
        ps_ref[...] = jnp.zeros_like(ps_ref)
        for r, c in enumerate(cols):
            ok = [((a + sub_iota) * CMP_STRIDE + (CMP_LEN - 1)) <= qpos for a, _ in _row_chunks(n_cmp)]
            m = neg_row
            for k, (a, b) in enumerate(_row_chunks(n_cmp)):
                m = jnp.maximum(m, jnp.max(jnp.where(ok[k], sc_ref[a:b, c], NEG_INF), axis=0, keepdims=True))
            den = jnp.zeros((1, Q_BLK), F32)
            for k, (a, b) in enumerate(_row_chunks(n_cmp)):
                e = jnp.where(ok[k], jnp.exp(sc_ref[a:b, c] - m), 0.0)
                den = den + jnp.sum(e, axis=0, keepdims=True)
                sc_ref[a:b, c] = e
            inv = 1.0 / jnp.where(den > 0, den, 1.0)
            for a, b in _row_chunks(n_cmp):
                p = sc_ref[a:b, c] * inv
                ps_ref[a:b, :] = ps_ref[a:b, :] + p
                p_ref[a:b, c] = p.astype(BF16)
        oc_t = _dot(vct, p_ref[0:n_cmp, :])
        for r, c in enumerate(cols):
            h = g * NSA_REP + r
            part_ref[:, c] = gate_t[h:h + 1, :] * oc_t[:, c]

        imp_t = jnp.dot(ov_t, ps_ref[...], preferred_element_type=F32, precision=lax.Precision.HIGHEST)
        sct = jnp.where(eligible_t, jnp.where(forced_t, FORCED_SCORE, imp_t), NEG_INF)
        sel_t = _rank_select(sct, eligible_t)
        sel_far = jnp.where((blk_t >> 1) == i, 0.0, sel_t)
        sel_bias = jnp.where(sel_far.T > 0.5, 0.0, NEG_INF).astype(BF16)
        for r in range(NSA_REP):
            qst_ref[r * Q_BLK:(r + 1) * Q_BLK, LANES:2 * LANES] = sel_bias

        own = pl.multiple_of(start, Q_BLK)
        sc_ref[0:Q_BLK, :] = _nt(ks_ref[pl.ds(own, Q_BLK), :], qst_ref[:, 0:LANES])
        tri = sub_iota <= _iota((Q_BLK, Q_BLK), 1)
        for c in cols:
            s_own = jnp.where(tri, sc_ref[0:Q_BLK, c], NEG_INF)
            m_own = jnp.max(s_own, axis=0, keepdims=True)
            p = jnp.exp(s_own - m_own)
            m_ref[:, c] = m_own
            l_ref[:, c] = jnp.sum(p, axis=0, keepdims=True)
            p_ref[0:Q_BLK, c] = p.astype(BF16)
        acc_ref[...] = _dot(vst_ref[:, pl.ds(own, Q_BLK)], p_ref[0:Q_BLK, :])

        def sel_body(kt, carry):
            k0 = pl.multiple_of(kt * SEL_TILE, SEL_TILE)
            ks = ks_ref[pl.ds(k0, SEL_TILE), :]
            vst = vst_ref[:, pl.ds(k0, SEL_TILE)]
            blk = kt * (SEL_TILE // SEL_LEN) + (_iota((SEL_TILE, LANES), 0) >> 6)
            onehot = jnp.where(_iota((SEL_TILE, LANES), 1) == blk, 1.0, 0.0).astype(BF16)
            kaug = jnp.concatenate([ks, onehot], axis=1)
            sc_ref[0:SEL_TILE, :] = _nt(kaug, qst_ref[...])
            for c in cols:
                m_old = m_ref[:, c]
                m_new = m_old
                for a, b in _row_chunks(SEL_TILE):
                    m_new = jnp.maximum(m_new, jnp.max(sc_ref[a:b, c], axis=0, keepdims=True))
                alpha = jnp.exp(m_old - m_new)
                ls = jnp.zeros((1, Q_BLK), F32)
                for a, b in _row_chunks(SEL_TILE):
                    p = jnp.exp(sc_ref[a:b, c] - m_new)
                    ls = ls + jnp.sum(p, axis=0, keepdims=True)
                    p_ref[a:b, c] = p.astype(BF16)
                l_ref[:, c] = alpha * l_ref[:, c] + ls
                m_ref[:, c] = m_new
                al_ref[:, c] = alpha
            acc_ref[...] = al_ref[...] * acc_ref[...] + _dot(vst, p_ref[0:SEL_TILE, :])
            return carry

        lax.fori_loop(0, n_kt, sel_body, 0)

        sc_ref[0:WIN_SPAN, :] = _nt(kw_ref[pl.ds(ws, WIN_SPAN), :], qst_ref[:, 0:LANES])
        for c in cols:
            ok = []
            for a, _ in _row_chunks(WIN_SPAN):
                dist = qpos - (ws + a + sub_iota)
                ok.append((dist >= 0) & (dist <= SLD_WINDOW))
            m = neg_row
            for k, (a, b) in enumerate(_row_chunks(WIN_SPAN)):
                m = jnp.maximum(m, jnp.max(jnp.where(ok[k], sc_ref[a:b, c], NEG_INF), axis=0, keepdims=True))
            den = jnp.zeros((1, Q_BLK), F32)
            for k, (a, b) in enumerate(_row_chunks(WIN_SPAN)):
                e = jnp.where(ok[k], jnp.exp(sc_ref[a:b, c] - m), 0.0)
                den = den + jnp.sum(e, axis=0, keepdims=True)
                p_ref[a:b, c] = e.astype(BF16)
            al_ref[:, c] = 1.0 / den
        ow_t = _dot(vwt_ref[:, pl.ds(ws, WIN_SPAN)], p_ref[0:WIN_SPAN, :]) * al_ref[...]

        for r, c in enumerate(cols):
            h = g * NSA_REP + r
            tot_t = (part_ref[:, c]
                     + gate_t[NSA_HEADS + h:NSA_HEADS + h + 1, :] * (acc_ref[:, c] / l_ref[:, c])
                     + gate_t[2 * NSA_HEADS + h:2 * NSA_HEADS + h + 1, :] * ow_t[:, c])
            tot = tot_t.T
            o_ref[:, h * HEAD_DIM:(h + 1) * HEAD_DIM] = tot[:, g * HEAD_DIM:(g + 1) * HEAD_DIM].astype(BF16)


def nsa_prompt(proj16, proj32, kcmp, vcmp_t, vslc_t, vwin_t, *, bsz, seq):
    n_i = seq // Q_BLK
    n_cmp = seq // CMP_STRIDE
    assert seq >= WIN_SPAN and seq % SEL_TILE == 0 and seq // SEL_LEN <= LANES and n_cmp % Q_BLK == 0
    kspec = lambda col: pl.BlockSpec((seq, LANES), lambda b, i: (b, col // LANES))
    vtspec = pl.BlockSpec((LANES, seq), lambda b, i: (b, 0))
    wide = NSA_REP * Q_BLK
    n_sc = max(n_cmp, SEL_TILE, WIN_SPAN)
    return pl.pallas_call(
        _nsa_prompt_kernel,
        grid=(bsz, n_i),
        in_specs=[pl.BlockSpec((Q_BLK, B_Q), lambda b, i: (b * n_i + i, 0)),
                  pl.BlockSpec((n_cmp, LANES), lambda b, i: (b, 0)),
                  pl.BlockSpec((LANES, n_cmp), lambda b, i: (b, 0)),
                  kspec(B_KSLC), vtspec, kspec(B_KWIN), vtspec,
                  pl.BlockSpec((Q_BLK, LANES), lambda b, i: (b * n_i + i, B_GATE // LANES))],
        out_specs=pl.BlockSpec((Q_BLK, B_Q), lambda b, i: (b * n_i + i, 0)),
        out_shape=jax.ShapeDtypeStruct((bsz * seq, B_Q), BF16),
        scratch_shapes=[pltpu.VMEM((wide, 2 * LANES), BF16),
                        pltpu.VMEM((n_sc, wide), F32),
                        pltpu.VMEM((n_sc, wide), BF16),
                        pltpu.VMEM((n_cmp, Q_BLK), F32),
                        pltpu.VMEM((1, wide), F32),
                        pltpu.VMEM((1, wide), F32),
                        pltpu.VMEM((1, wide), F32),
                        pltpu.VMEM((LANES, wide), F32),
                        pltpu.VMEM((LANES, wide), F32)],
        compiler_params=_cp(("parallel", "arbitrary"), VMEM_LIMIT),
        name="nsa_prompt",
    )(proj16, kcmp, vcmp_t, proj16, vslc_t, proj16, vwin_t, proj32)


def _nsa_sample_kernel(qpad_ref, kc_ref, vc_ref, past_ref, new_ref, win_ref, gate_ref, o_ref,
                       *, n_new, n_past):
    rows = qpad_ref.shape[0]
    rho = _iota((rows, 1), 0)
    t_row = (rho >> 3) % n_new
    qpos = n_past + t_row
    qpad = qpad_ref[...]
    gate = 1.0 / (1.0 + jnp.exp(-gate_ref[...]))
    n_pad = new_ref.shape[0]
    tnew = _iota((rows, n_pad), 1)
    new = new_ref[...]
    new16 = new.astype(BF16)
    new_r = new16.astype(F32)

    def attend(s_past, ok_past, v_past16, s_new, ok_new, v_new_r):
        s_past = jnp.where(ok_past, s_past, NEG_INF)
        s_new = jnp.where(ok_new, s_new, NEG_INF)
        m = jnp.maximum(jnp.max(s_past, axis=-1, keepdims=True), jnp.max(s_new, axis=-1, keepdims=True))
        e_past = jnp.where(ok_past, jnp.exp(s_past - m), 0.0)
        e_new = jnp.where(ok_new, jnp.exp(s_new - m), 0.0)
        den = jnp.sum(e_past, axis=-1, keepdims=True) + jnp.sum(e_new, axis=-1, keepdims=True)
        o = _dot(e_past.astype(BF16), v_past16)
        e_new_r = e_new.astype(BF16).astype(F32)
        for t in range(n_new):
            o = o + e_new_r[:, t:t + 1] * v_new_r[t:t + 1, :]
        return o / den

    n_cmp = kc_ref.shape[0]
    ok_c = (_iota((rows, n_cmp), 1) * CMP_STRIDE + (CMP_LEN - 1)) <= qpos
    s = jnp.where(ok_c, _nt(qpad, kc_ref[...].astype(BF16)), NEG_INF)
    m = jnp.max(s, axis=-1, keepdims=True)
    e = jnp.where(ok_c, jnp.exp(s - m), 0.0)
    den = jnp.sum(e, axis=-1, keepdims=True)
    p = e / jnp.where(den > 0, den, 1.0)
    o_cmp = _dot(p.astype(BF16), vc_ref[...].astype(BF16))

    same = jnp.where((_iota((rows, rows), 0) >> 3) == (_iota((rows, rows), 1) >> 3), 1.0, 0.0)
    psum = jnp.dot(same, p, preferred_element_type=F32, precision=lax.Precision.HIGHEST)
    imp = jnp.dot(psum, _overlap(n_cmp, LANES), preferred_element_type=F32, precision=lax.Precision.HIGHEST)
    score, eligible = _block_scores(imp, qpos)
    n_sel = -(-(n_past + n_new) // SEL_LEN)
    jcol = _iota(score.shape, 1)
    cnt = jnp.zeros(score.shape, F32)
    for i2 in range(n_sel):
        col = score[:, i2:i2 + 1]
        cnt = cnt + jnp.where(col > score, 1.0, jnp.where(col == score, jnp.where(jcol > i2, 1.0, 0.0), 0.0))
    sel = jnp.where(eligible, jnp.where(cnt < SEL_TOPK, 1.0, 0.0), 0.0)

    expand = jnp.where(_iota((LANES, n_past), 0) == (_iota((LANES, n_past), 1) >> 6), 1.0, 0.0).astype(BF16)
    selk = _dot(sel.astype(BF16), expand)
    past16 = past_ref[...].astype(BF16)
    blk_new = n_past // SEL_LEN
    ok_new_sel = (sel[:, blk_new:blk_new + 1] > 0.5) & (tnew <= t_row)
    o_sel = attend(_nt(qpad, past16[:, :LANES]), selk > 0.5, past16[:, LANES:],
                   _nt(qpad, new16[:, :LANES]), ok_new_sel, new_r[:, LANES:2 * LANES])

    win16 = win_ref[...].astype(BF16)
    n_win = win16.shape[0]
    ok_w = _iota((rows, n_win), 1) >= t_row + (n_win - SLD_WINDOW)
    o_win = attend(_nt(qpad, win16[:, :LANES]), ok_w, win16[:, LANES:],
                   _nt(qpad, new16[:, 2 * LANES:3 * LANES]), tnew <= t_row, new_r[:, 3 * LANES:])

    o = gate[:, 0:1] * o_cmp + gate[:, 1:2] * o_sel + gate[:, 2:3] * o_win
    o_ref[...] = jnp.where(rho < rows // 2, o[:, :HEAD_DIM], o[:, HEAD_DIM:])


def nsa_sample(qpad, kcmp, vcmp, past2d, newkv, win_cache, gates, layer, *, n_new, n_past):
    nb, rows, _ = qpad.shape
    n_cmp = n_past // CMP_STRIDE
    n_win = win_cache.shape[2]
    assert n_past % SEL_LEN == 0 and n_new <= SEL_LEN and n_win >= SLD_WINDOW
    return pl.pallas_call(
        functools.partial(_nsa_sample_kernel, n_new=n_new, n_past=n_past),
        grid=(nb,),
        in_specs=[pl.BlockSpec((None, rows, LANES), lambda b: (b, 0, 0)),
                  pl.BlockSpec((n_cmp, LANES), lambda b: (b, 0)),
                  pl.BlockSpec((n_cmp, LANES), lambda b: (b, 0)),
                  pl.BlockSpec((n_past, 2 * LANES), lambda b: (b, 1)),
                  pl.BlockSpec((None,) + newkv.shape[1:], lambda b: (b, 0, 0)),
                  pl.BlockSpec((None, None, n_win, 2 * LANES), lambda b: (layer, b, 0, 0)),
                  pl.BlockSpec((None, rows, LANES), lambda b: (b, 0, 0))],
        out_specs=pl.BlockSpec((None, rows, HEAD_DIM), lambda b: (b, 0, 0)),
        out_shape=jax.ShapeDtypeStruct((nb, rows, HEAD_DIM), F32),
        compiler_params=_cp(("parallel",), VMEM_LIMIT),
        name="nsa_sample",
    )(qpad, kcmp, vcmp, past2d, newkv, win_cache, gates)


def _rope_tables(pos):
    inv = ROPE_THETA ** (-jnp.arange(HALF, dtype=F32) / HALF)
    ang = pos.astype(F32)[:, None] * inv[None, :]
    cos, sin, zero = jnp.cos(ang), jnp.sin(ang), jnp.zeros_like(ang)
    return (jnp.concatenate([cos, cos] * 2, axis=1),
            jnp.concatenate([-sin, zero] * 2, axis=1),
            jnp.concatenate([zero, sin] * 2, axis=1))


def _column_roles_a():
    slab = (np.arange(3 * 3 * DIL_W) % (3 * DIL_W)) // DIL_W
    rope = (slab < 2).astype(np.float32)
    scale = np.where(slab == 0, ATTN_SCALE, 1.0).astype(np.float32)
    return rope[None, :], scale[None, :]


def _column_roles_b():
    col = np.arange(B_COLS)
    is_q = col < B_Q
    is_k = (col >= B_KCMP) & (col < B_GATE) & ((col - B_KCMP) % (2 * LANES) < LANES)
    rope = (is_q | is_k).astype(np.float32)
    scale = np.where(is_q, ATTN_SCALE, 1.0).astype(np.float32)
    return rope[None, :], scale[None, :]


def _block_diag2(w):
    z = jnp.zeros_like(w)
    return jnp.concatenate([jnp.concatenate([w, z], axis=-1), jnp.concatenate([z, w], axis=-1)], axis=-2)


def kernel(x_prompt, x_sample, p_prompt, p_sample, cache_a_w128, cache_a_w512, cache_a_w2048, cache_b_kv, cache_b_win, page_table, norm_mix, norm_ffn, norm_ple, norm_final, a_w_in, a_w_out, b_w_in, b_w_out, b_cmp_pe, b_cmp_w1, b_cmp_w2, moe_w_group, moe_b_group, moe_w_expert, moe_b_expert, moe_w_gate, moe_w_up, moe_w_down, ple_w_proj, ple_w_gate):
    bsz, seq, d = x_prompt.shape
    nb, n_new, _ = x_sample.shape
    depth = norm_mix.shape[0]
    n_pages, page = page_table.shape[1], cache_b_kv.shape[2]
    n_past = n_pages * page
    np_ = bsz * seq
    ns = nb * n_new

    h = jnp.concatenate([x_prompt.reshape(np_, d), x_sample.reshape(ns, d)], axis=0)
    pos = jnp.concatenate([jnp.tile(jnp.arange(seq, dtype=jnp.int32), bsz),
                           jnp.tile(n_past + jnp.arange(n_new, dtype=jnp.int32), nb)])
    rope = _rope_tables(pos)
    roles_a = tuple(jnp.asarray(a) for a in _column_roles_a())
    roles_b = tuple(jnp.asarray(a) for a in _column_roles_b())

    caches_t = [jnp.transpose(c, (0, 1, 3, 4, 5, 2)).reshape(c.shape[0], nb, 2, DIL_W, c.shape[2])
                for c in (cache_a_w128, cache_a_w512, cache_a_w2048)]
    a_upd = None
    kv_w = cache_b_kv.shape[3] * cache_b_kv.shape[4] * cache_b_kv.shape[5]
    cache_kv4 = cache_b_kv.reshape(cache_b_kv.shape[0], cache_b_kv.shape[1], page, kv_w)
    cache_win4 = cache_b_win.reshape(cache_b_win.shape[0], nb, cache_b_win.shape[2], 2 * LANES)

    a_bufs_p, b_rows_p, b_rows_s, b_win_p, b_win_s = [], [], [], [], []
    for i in range(depth):
        li = i // 2
        if i % 2 == 0:
            qkv32, qkv16 = norm_proj(h, norm_mix[i], a_w_in[li].astype(BF16), *roles_a, *rope)
            parts = [dil_prompt(qkv16, g, bsz=bsz, seq=seq) for g in range(3)]
            o_p = dil_merge([p[0] for p in parts], [p[1] for p in parts])
            qkv_new8 = jnp.pad(qkv32[np_:].reshape(nb, n_new, qkv32.shape[1]), ((0, 0), (NEW_PAD - n_new, 0), (0, 0)))
            o_s, a_upd = dil_sample(qkv_new8, caches_t, a_upd, li, n_new=n_new)
            o16 = jnp.concatenate([o_p, o_s.reshape(ns, DIL_W).astype(BF16)], axis=0)
            h = out_proj(h, o16, a_w_out[li].astype(BF16))
            qkv_p = qkv32[:np_].reshape(bsz, seq, qkv32.shape[1])
            bp = []
            for g in range(3):
                lo, hi = g * 3 * DIL_W + DIL_W, (g + 1) * 3 * DIL_W
                keep = min(DIL_WINDOWS[g], seq)
                bp.append(qkv_p[:, seq - keep:, lo:hi].reshape(bsz, keep, 2, DIL_HEADS, HEAD_DIM))
            a_bufs_p.append(bp)
        else:
            w_in = jnp.pad(b_w_in[li], ((0, 0), (0, B_COLS - b_w_in.shape[2]))).astype(BF16)
            p32, p16 = norm_proj(h, norm_mix[i], w_in, *roles_b, *rope)
            pe_pair = jnp.concatenate([b_cmp_pe[li], b_cmp_pe[li]], axis=-1)
            w1bd = _block_diag2(b_cmp_w1[li].reshape(2, CMP_LEN, HEAD_DIM, CMP_HIDDEN)).astype(BF16)
            w2bd = _block_diag2(b_cmp_w2[li]).astype(BF16)
            kc_p, vc_p = compress(p32, B_KCMP // (2 * LANES), bsz, pe_pair, w1bd, w2bd, rows=seq)
            n_cmp = seq // CMP_STRIDE
            vc_t = vc_p.reshape(bsz, n_cmp, LANES).transpose(0, 2, 1).reshape(bsz * LANES, n_cmp)
            pp16 = p16[:np_].reshape(bsz, seq, B_COLS)
            vs_t = pp16[:, :, B_VSLC:B_VSLC + LANES].transpose(0, 2, 1).reshape(bsz * LANES, seq)
            vw_t = pp16[:, :, B_VWIN:B_VWIN + LANES].transpose(0, 2, 1).reshape(bsz * LANES, seq)
            o_p = nsa_prompt(p16, p32, kc_p, vc_t, vs_t, vw_t, bsz=bsz, seq=seq)
            past = gather_pages(cache_kv4, page_table, li)
            step_rows = max(n_past, 8192)
            kc_s, vc_s = compress(past, 0, nb * n_past // step_rows, pe_pair, w1bd, w2bd, rows=step_rows)
            ps32, ps16 = p32[np_:], p16[np_:]
            q = ps16[:, :B_Q].reshape(nb, n_new, 2, NSA_REP, HEAD_DIM).transpose(0, 2, 1, 3, 4)
            zq = jnp.zeros_like(q[:, 0])
            qpad = jnp.stack([jnp.concatenate([q[:, 0], zq], axis=-1),
                              jnp.concatenate([zq, q[:, 1]], axis=-1)], axis=1)
            qpad = qpad.reshape(nb, 2 * n_new * NSA_REP, LANES)
            gts = ps32[:, B_GATE:B_GATE + 3 * NSA_HEADS].reshape(nb, n_new, 3, 2, NSA_REP)
            gts = gts.transpose(0, 3, 1, 4, 2).reshape(nb, 2 * n_new * NSA_REP, 3)
            gts = jnp.pad(gts, ((0, 0), (0, 0), (0, LANES - 3)))
            newkv = ps32[:, B_KSLC:B_GATE].reshape(nb, n_new, B_GATE - B_KSLC)
            newkv = jnp.pad(newkv, ((0, 0), (0, 8 - n_new), (0, 0)))
            o_s = nsa_sample(qpad, kc_s, vc_s, past, newkv, cache_win4, gts, li, n_new=n_new, n_past=n_past)
            o_s = o_s.reshape(nb, 2, n_new, NSA_REP, HEAD_DIM).transpose(0, 2, 1, 3, 4).reshape(ns, B_Q)
            o16 = jnp.concatenate([o_p, o_s.astype(BF16)], axis=0)
            h = out_proj(h, o16, b_w_out[li].astype(BF16))
            pp = p32[:np_].reshape(bsz, seq, B_COLS)
            keep = min(SLD_WINDOW, seq)
            b_rows_p.append(pp[:, :, B_KCMP:B_KWIN].reshape(bsz, seq, 4, 2, HEAD_DIM))
            b_win_p.append(pp[:, seq - keep:, B_KWIN:B_GATE].reshape(bsz, keep, 2, 2, HEAD_DIM))
            b_rows_s.append(ps32[:, B_KCMP:B_KWIN].reshape(nb, n_new, 4, 2, HEAD_DIM))
            new_win = ps32[:, B_KWIN:B_GATE].reshape(nb, n_new, 2, 2, HEAD_DIM)
            b_win_s.append(jnp.concatenate([cache_b_win[li][:, n_new:], new_win], axis=1))

        w_route = jnp.pad(jnp.concatenate([moe_w_group[i], moe_w_expert[i]], axis=1),
                          ((0, 0), (0, LANES - N_GROUPS - N_EXPERTS)))
        b_route = jnp.pad(jnp.concatenate([moe_b_group[i], moe_b_expert[i]]),
                          (0, LANES - N_GROUPS - N_EXPERTS)).reshape(1, LANES)
        h = moe(h, norm_ffn[i], w_route, b_route, moe_w_gate[i].astype(BF16),
                moe_w_up[i].astype(BF16), moe_w_down[i].astype(BF16))
        p_tok = jnp.concatenate([p_prompt[i].reshape(np_, -1), p_sample[i].reshape(ns, -1)], axis=0)
        h = ple(h, norm_ple[i], p_tok, ple_w_gate[i].astype(BF16), ple_w_proj[i].astype(BF16))

    y = rms_norm(h, norm_final)
    a_state = [jnp.transpose(u.reshape(u.shape[0], nb, 2, DIL_HEADS, HEAD_DIM, u.shape[4]), (0, 1, 5, 2, 3, 4))
               for u in a_upd]
    stack = lambda bufs, g: jnp.stack([b[g] for b in bufs], axis=0)
    return (y[:np_].reshape(bsz, seq, d), y[np_:].reshape(nb, n_new, d),
            stack(a_bufs_p, 0), stack(a_bufs_p, 1), stack(a_bufs_p, 2),
            jnp.stack(b_rows_p, axis=0), jnp.stack(b_win_p, axis=0),
            *a_state,
            jnp.stack(b_rows_s, axis=0), jnp.stack(b_win_s, axis=0))
```

```python
import functools

import numpy as np
import jax
import jax.numpy as jnp
from jax import lax
from jax.experimental import pallas as pl
from jax.experimental.pallas import tpu as pltpu

F32 = jnp.float32
BF16 = jnp.bfloat16

HEAD_DIM = 64
HALF = HEAD_DIM // 2
ROPE_THETA = 10000.0
RMS_EPS = 1e-6
NEG_INF = -1e30
ATTN_SCALE = HEAD_DIM ** -0.5
LOG2E = 1.4426950408889634
LANES = 128
Q_BLK = 128

DIL_WINDOWS = (128, 512, 2048)
DIL_RATES = (1, 4, 16)
DIL_HEADS = 8
DIL_W = DIL_HEADS * HEAD_DIM

NSA_HEADS = 16
NSA_REP = 8
CMP_LEN = 32
CMP_STRIDE = 16
CMP_HIDDEN = 256
SEL_LEN = 64
SEL_TOPK = 16
SLD_WINDOW = 512
FORCED_SCORE = 1e9
B_COLS = 2048
B_Q = 1024
B_KCMP, B_VCMP, B_KSLC, B_VSLC, B_KWIN, B_VWIN, B_GATE = (1024, 1152, 1280, 1408, 1536, 1664, 1792)

N_GROUPS = 4
EXP_PER_GROUP = 8
N_EXPERTS = 32

VMEM_LIMIT = 56 * 1024 * 1024


def _cp(sem, vmem=None):
    return pltpu.CompilerParams(dimension_semantics=sem, vmem_limit_bytes=vmem)


def _nt(a, b):
    return lax.dot_general(a, b, (((1,), (1,)), ((), ())), preferred_element_type=F32)


def _dot(a, b):
    return jnp.dot(a, b, preferred_element_type=F32)


def _iota(shape, dim):
    return lax.broadcasted_iota(jnp.int32, shape, dim)


def _norm_proj_kernel(x_ref, g_ref, w_ref, rc_ref, sc_ref, c_ref, s1_ref, s2_ref,
                      o32_ref, o16_ref, xn_ref, *, tn):
    @pl.when(pl.program_id(1) == 0)
    def _():
        x = x_ref[...]
        var = jnp.mean(x * x, axis=-1, keepdims=True)
        xn_ref[...] = (x * lax.rsqrt(var + RMS_EPS) * g_ref[...]).astype(BF16)

    y = _dot(xn_ref[...], w_ref[...])
    reps = tn // LANES
    rc = rc_ref[...]
    c = 1.0 + rc * (jnp.tile(c_ref[...], (1, reps)) - 1.0)
    s1 = rc * jnp.tile(s1_ref[...], (1, reps))
    s2 = rc * jnp.tile(s2_ref[...], (1, reps))
    y = y * c + pltpu.roll(y, tn - HALF, 1) * s1 + pltpu.roll(y, HALF, 1) * s2
    o32_ref[...] = y
    o16_ref[...] = (y * sc_ref[...]).astype(BF16)


def norm_proj(x, g, w16, ropecol, colscale, rope_c, rope_s1, rope_s2, *, tm=512, tn=512):
    n, d = x.shape
    ncols = w16.shape[1]
    assert n % tm == 0 and ncols % tn == 0
    return pl.pallas_call(
        functools.partial(_norm_proj_kernel, tn=tn),
        grid=(n // tm, ncols // tn),
        in_specs=[
            pl.BlockSpec((tm, d), lambda i, j: (i, 0)),
            pl.BlockSpec((1, d), lambda i, j: (0, 0)),
            pl.BlockSpec((d, tn), lambda i, j: (0, j)),
            pl.BlockSpec((1, tn), lambda i, j: (0, j)),
            pl.BlockSpec((1, tn), lambda i, j: (0, j)),
            pl.BlockSpec((tm, LANES), lambda i, j: (i, 0)),
            pl.BlockSpec((tm, LANES), lambda i, j: (i, 0)),
            pl.BlockSpec((tm, LANES), lambda i, j: (i, 0)),
        ],
        out_specs=[pl.BlockSpec((tm, tn), lambda i, j: (i, j)),
                   pl.BlockSpec((tm, tn), lambda i, j: (i, j))],
        out_shape=[jax.ShapeDtypeStruct((n, ncols), F32),
                   jax.ShapeDtypeStruct((n, ncols), BF16)],
        scratch_shapes=[pltpu.VMEM((tm, d), BF16)],
        compiler_params=_cp(("parallel", "arbitrary")),
        name="norm_proj",
    )(x, g.reshape(1, d), w16, ropecol, colscale, rope_c, rope_s1, rope_s2)


def _out_proj_kernel(h_ref, o_ref, w_ref, out_ref):
    out_ref[...] = h_ref[...] + _dot(o_ref[...], w_ref[...])


def out_proj(h, o16, w16, *, tm=512):
    n, d = h.shape
    k = o16.shape[1]
    return pl.pallas_call(
        _out_proj_kernel,
        grid=(n // tm,),
        in_specs=[pl.BlockSpec((tm, d), lambda i: (i, 0)),
                  pl.BlockSpec((tm, k), lambda i: (i, 0)),
                  pl.BlockSpec((k, d), lambda i: (0, 0))],
        out_specs=pl.BlockSpec((tm, d), lambda i: (i, 0)),
        out_shape=jax.ShapeDtypeStruct((n, d), F32),
        compiler_params=_cp(("parallel",)),
        name="out_proj",
    )(h, o16, w16)


def _moe_kernel(h_ref, g_ref, wr_ref, br_ref, wg_ref, wu_ref, wd_ref, out_ref,
                xn_ref, comb_ref, acc_ref):
    grp = pl.program_id(1)

    @pl.when(grp == 0)
    def _():
        x = h_ref[...]
        var = jnp.mean(x * x, axis=-1, keepdims=True)
        xn = x * lax.rsqrt(var + RMS_EPS) * g_ref[...]
        xn_ref[...] = xn.astype(BF16)
        logit = jnp.dot(xn, wr_ref[...], preferred_element_type=F32,
                        precision=lax.Precision.HIGHEST) + br_ref[...]
        lane = _iota(logit.shape, 1)
        is_grp = lane < N_GROUPS
        lg = jnp.where(is_grp, logit, -jnp.inf)
        gmax = jnp.max(lg, axis=-1, keepdims=True)
        gidx = jnp.min(jnp.where(lg == gmax, lane, LANES), axis=-1, keepdims=True)
        p_grp = 1.0 / jnp.sum(jnp.where(is_grp, jnp.exp(logit - gmax), 0.0), axis=-1, keepdims=True)
        lo = N_GROUPS + gidx * EXP_PER_GROUP
        in_grp = (lane >= lo) & (lane < lo + EXP_PER_GROUP)
        le = jnp.where(in_grp, logit, -jnp.inf)
        v1 = jnp.max(le, axis=-1, keepdims=True)
        i1 = jnp.min(jnp.where(le == v1, lane, LANES), axis=-1, keepdims=True)
        le2 = jnp.where(lane == i1, -jnp.inf, le)
        v2 = jnp.max(le2, axis=-1, keepdims=True)
        i2 = jnp.min(jnp.where(le2 == v2, lane, LANES), axis=-1, keepdims=True)
        e2 = jnp.exp(v2 - v1)
        w1 = p_grp / (1.0 + e2)
        w2 = p_grp * e2 / (1.0 + e2)
        comb = jnp.where(lane == i1, w1, jnp.where(lane == i2, w2, 0.0))
        for gg in range(N_GROUPS):
            sel = jnp.where(gidx == gg, comb, 0.0)
            comb_ref[gg] = pltpu.roll(sel, LANES - (N_GROUPS + gg * EXP_PER_GROUP), 1)
        acc_ref[...] = jnp.zeros_like(acc_ref)

    xn = xn_ref[...]
    comb = comb_ref[grp]
    acc = acc_ref[...]
    for k in range(EXP_PER_GROUP):
        hg = _dot(xn, wg_ref[k])
        hu = _dot(xn, wu_ref[k])
        a = (hg / (1.0 + jnp.exp(-hg))) * hu * comb[:, k:k + 1]
        acc = acc + _dot(a.astype(BF16), wd_ref[k])
    acc_ref[...] = acc

    @pl.when(grp == N_GROUPS - 1)
    def _():
        out_ref[...] = h_ref[...] + acc_ref[...]


def moe(h, g, w_route, b_route, wg16, wu16, wd16, *, tm=512):
    n, d = h.shape
    f = wg16.shape[2]
    return pl.pallas_call(
        _moe_kernel,
        grid=(n // tm, N_GROUPS),
        in_specs=[
            pl.BlockSpec((tm, d), lambda i, e: (i, 0)),
            pl.BlockSpec((1, d), lambda i, e: (0, 0)),
            pl.BlockSpec((d, LANES), lambda i, e: (0, 0)),
            pl.BlockSpec((1, LANES), lambda i, e: (0, 0)),
            pl.BlockSpec((EXP_PER_GROUP, d, f), lambda i, e: (e, 0, 0)),
            pl.BlockSpec((EXP_PER_GROUP, d, f), lambda i, e: (e, 0, 0)),
            pl.BlockSpec((EXP_PER_GROUP, f, d), lambda i, e: (e, 0, 0)),
        ],
        out_specs=pl.BlockSpec((tm, d), lambda i, e: (i, 0)),
        out_shape=jax.ShapeDtypeStruct((n, d), F32),
        scratch_shapes=[pltpu.VMEM((tm, d), BF16),
                        pltpu.VMEM((N_GROUPS, tm, LANES), F32),
                        pltpu.VMEM((tm, d), F32)],
        compiler_params=_cp(("parallel", "arbitrary"), VMEM_LIMIT),
        name="moe",
    )(h, g.reshape(1, d), w_route, b_route, wg16, wu16, wd16)


def _ple_kernel(h_ref, g_ref, p_ref, wgate_ref, wproj_ref, out_ref):
    x = h_ref[...]
    var = jnp.mean(x * x, axis=-1, keepdims=True)
    xn = (x * lax.rsqrt(var + RMS_EPS) * g_ref[...]).astype(BF16)
    gate = 1.0 / (1.0 + jnp.exp(-_dot(xn, wgate_ref[...])))
    out_ref[...] = x + gate * _dot(p_ref[...].astype(BF16), wproj_ref[...])


def ple(h, g, p, wgate16, wproj16, *, tm=512):
    n, d = h.shape
    pd = p.shape[1]
    return pl.pallas_call(
        _ple_kernel,
        grid=(n // tm,),
        in_specs=[pl.BlockSpec((tm, d), lambda i: (i, 0)),
                  pl.BlockSpec((1, d), lambda i: (0, 0)),
                  pl.BlockSpec((tm, pd), lambda i: (i, 0)),
                  pl.BlockSpec((d, d), lambda i: (0, 0)),
                  pl.BlockSpec((pd, d), lambda i: (0, 0))],
        out_specs=pl.BlockSpec((tm, d), lambda i: (i, 0)),
        out_shape=jax.ShapeDtypeStruct((n, d), F32),
        compiler_params=_cp(("parallel",)),
        name="ple",
    )(h, g.reshape(1, d), p, wgate16, wproj16)


def _rms_kernel(h_ref, g_ref, out_ref):
    x = h_ref[...]
    var = jnp.mean(x * x, axis=-1, keepdims=True)
    out_ref[...] = x * lax.rsqrt(var + RMS_EPS) * g_ref[...]


def rms_norm(h, g, *, tm=512):
    n, d = h.shape
    return pl.pallas_call(
        _rms_kernel,
        grid=(n // tm,),
        in_specs=[pl.BlockSpec((tm, d), lambda i: (i, 0)),
                  pl.BlockSpec((1, d), lambda i: (0, 0))],
        out_specs=pl.BlockSpec((tm, d), lambda i: (i, 0)),
        out_shape=jax.ShapeDtypeStruct((n, d), F32),
        compiler_params=_cp(("parallel",)),
        name="final_norm",
    )(h, g.reshape(1, d))


def _dil_prompt_kernel(q_ref, kp_ref, kc_ref, vp_ref, vc_ref, o_ref, lse_ref):
    i = pl.program_id(2)
    q = q_ref[...]
    kk = jnp.concatenate([kp_ref[...], kc_ref[...]], axis=0)
    vv = jnp.concatenate([vp_ref[...], vc_ref[...]], axis=0)
    qu = i * Q_BLK + _iota((Q_BLK, 2 * Q_BLK), 0)
    ku = (i - 1) * Q_BLK + _iota((Q_BLK, 2 * Q_BLK), 1)
    dist = qu - ku
    valid = (ku >= 0) & (dist >= 0) & (dist <= Q_BLK)
    for h in range(DIL_HEADS):
        sl = slice(h * HEAD_DIM, (h + 1) * HEAD_DIM)
        s = jnp.where(valid, _nt(q[:, sl], kk[:, sl]), NEG_INF)
        m = jnp.max(s, axis=-1, keepdims=True)
        e = jnp.exp(s - m)
        den = jnp.sum(e, axis=-1, keepdims=True)
        o = _dot(e.astype(BF16), vv[:, sl]) / den
        o_ref[:, sl] = o
        lse_ref[:, sl] = jnp.broadcast_to(m + jnp.log(den), (Q_BLK, HEAD_DIM))


def dil_prompt(qkv16, g, *, bsz, seq):
    r = DIL_RATES[g]
    if r == 1:
        view, nslab, first = qkv16, qkv16.shape[1] // DIL_W, 3 * g
    else:
        mine = qkv16[:, 3 * g * DIL_W:3 * (g + 1) * DIL_W]
        view, nslab, first = mine.reshape(mine.shape[0] // r, r * 3 * DIL_W), 3, 0
    n_i = seq // r // Q_BLK
    assert DIL_WINDOWS[g] // r == Q_BLK and seq % (r * Q_BLK) == 0

    def spec(slab, prev):
        def imap(b, c, i):
            ii = jnp.maximum(i - 1, 0) if prev else i
            return (b * n_i + ii, c * nslab + first + slab)
        return pl.BlockSpec((Q_BLK, DIL_W), imap)

    out_spec = pl.BlockSpec((Q_BLK, DIL_W), lambda b, c, i: (b * n_i + i, c))
    shp = jax.ShapeDtypeStruct((bsz * seq // r, r * DIL_W), F32)
    o, lse = pl.pallas_call(
        _dil_prompt_kernel,
        grid=(bsz, r, n_i),
        in_specs=[spec(0, False), spec(1, True), spec(1, False), spec(2, True), spec(2, False)],
        out_specs=[out_spec, out_spec],
        out_shape=[shp, shp],
        compiler_params=_cp(("parallel", "parallel", "arbitrary")),
        name=f"dil_prompt_g{g}",
    )(view, view, view, view, view)
    return o.reshape(bsz * seq, DIL_W), lse.reshape(bsz * seq, DIL_W)


def _dil_merge_kernel(o0, o1, o2, l0, l1, l2, out_ref):
    a, b, c = l0[...], l1[...], l2[...]
    m = jnp.maximum(jnp.maximum(a, b), c)
    wa, wb, wc = jnp.exp(a - m), jnp.exp(b - m), jnp.exp(c - m)
    out = (wa * o0[...] + wb * o1[...] + wc * o2[...]) / (wa + wb + wc)
    out_ref[...] = out.astype(BF16)


def dil_merge(os_, lses, *, tm=1024):
    n = os_[0].shape[0]
    spec = pl.BlockSpec((tm, DIL_W), lambda i: (i, 0))
    return pl.pallas_call(
        _dil_merge_kernel,
        grid=(n // tm,),
        in_specs=[spec] * 6,
        out_specs=spec,
        out_shape=jax.ShapeDtypeStruct((n, DIL_W), BF16),
        compiler_params=_cp(("parallel",)),
        name="dil_merge",
    )(*os_, *lses)


NEW_PAD = 8


def _dil_sample_kernel(*refs, n_new, n_alias):
    qn_ref, c0, c1, c2 = refs[:4]
    o_ref, u0, u1, u2, e0, e1, e2, en_ref, st_ref = refs[4 + n_alias:]
    caches, ups, es = (c0, c1, c2), (u0, u1, u2), (e0, e1, e2)
    kv = pl.program_id(1)
    rows = n_new * DIL_HEADS
    off = NEW_PAD - n_new
    bd = jnp.where((_iota((DIL_HEADS, DIL_W), 1) >> 6) == _iota((DIL_HEADS, DIL_W), 0), 1.0, 0.0)
    t_row = _iota((rows, 1), 0) >> 3
    lane = _iota((DIL_W, LANES), 1)

    def shifted(x, new8):
        w = x.shape[1]
        new_t = jnp.concatenate([jnp.zeros((LANES - NEW_PAD, DIL_W), F32), new8], axis=0).T
        rolled = pltpu.roll(x, w - n_new, 1)
        tail = jnp.where(lane >= LANES - n_new, new_t, rolled[:, w - LANES:])
        return tail if w == LANES else jnp.concatenate([rolled[:, :w - LANES], tail], axis=1)

    @pl.when(kv == 0)
    def _():
        for g in range(3):
            rate = DIL_RATES[g]
            base = g * 3 * DIL_W
            qg = qn_ref[:, base:base + DIL_W] * ATTN_SCALE
            kn = qn_ref[:, base + DIL_W:base + 2 * DIL_W]
            qh = jnp.concatenate([qg[off + t:off + t + 1, :] * bd for t in range(n_new)], axis=0).astype(BF16)
            x = caches[g][...]
            w = x.shape[1]
            s_past = _dot(qh, x.astype(BF16))
            gap = _iota((rows, w), 1) - t_row
            ok_past = (gap >= 0) & ((gap & (rate - 1)) == 0)
            s_new = _nt(qh, kn.astype(BF16))
            gap_n = t_row - (_iota((rows, NEW_PAD), 1) - off)
            ok_new = (gap_n >= 0) & (gap_n <= t_row) & ((gap_n & (rate - 1)) == 0)
            s_past = jnp.where(ok_past, s_past, NEG_INF)
            s_new = jnp.where(ok_new, s_new, NEG_INF)
            m = jnp.maximum(jnp.max(s_past, axis=-1, keepdims=True), jnp.max(s_new, axis=-1, keepdims=True))
            e_past = jnp.where(ok_past, jnp.exp(s_past - m), 0.0)
            e_new = jnp.where(ok_new, jnp.exp(s_new - m), 0.0)
            den = jnp.sum(e_past, axis=-1, keepdims=True) + jnp.sum(e_new, axis=-1, keepdims=True)
            es[g][...] = e_past.astype(BF16)
            en_ref[g] = e_new
            st_ref[g, 0] = m + jnp.log(den)
            st_ref[g, 1] = 1.0 / den
            ups[g][...] = shifted(x, kn)

    @pl.when(kv == 1)
    def _():
        outs, lses = [], []
        for g in range(3):
            base = g * 3 * DIL_W
            vn = qn_ref[:, base + 2 * DIL_W:base + 3 * DIL_W]
            x = caches[g][...]
            o = _nt(es[g][...], x.astype(BF16))
            e_new = en_ref[g].astype(BF16).astype(F32)
            vn_r = vn.astype(BF16).astype(F32)
            for t in range(n_new):
                o = o + e_new[:, off + t:off + t + 1] * vn_r[off + t:off + t + 1, :]
            outs.append(o * st_ref[g, 1])
            lses.append(st_ref[g, 0])
            ups[g][...] = shifted(x, vn)
        mm = jnp.maximum(jnp.maximum(lses[0], lses[1]), lses[2])
        ws = [jnp.exp(l - mm) for l in lses]
        tot = (ws[0] * outs[0] + ws[1] * outs[1] + ws[2] * outs[2]) / (ws[0] + ws[1] + ws[2])
        tot = tot * jnp.concatenate([bd] * n_new, axis=0)
        o_ref[...] = jnp.concatenate(
            [jnp.sum(tot[t * DIL_HEADS:(t + 1) * DIL_HEADS], axis=0, keepdims=True) for t in range(n_new)], axis=0)


def dil_sample(qkv_new8, caches_t, prev, layer, *, n_new):
    nb, _, ncol = qkv_new8.shape
    for g, c in enumerate(caches_t):
        assert c.shape[4] == DIL_WINDOWS[g] and DIL_RATES[g] & (DIL_RATES[g] - 1) == 0
    rows = n_new * DIL_HEADS
    cspec = lambda c: pl.BlockSpec((None, None, None, DIL_W, c.shape[4]), lambda b, kv: (layer, b, kv, 0, 0))
    n_alias = 0 if prev is None else 3
    any_spec = pl.BlockSpec(memory_space=pl.ANY)
    outs = pl.pallas_call(
        functools.partial(_dil_sample_kernel, n_new=n_new, n_alias=n_alias),
        grid=(nb, 2),
        in_specs=[pl.BlockSpec((None, NEW_PAD, ncol), lambda b, kv: (b, 0, 0))]
                 + [cspec(c) for c in caches_t] + [any_spec] * n_alias,
        out_specs=[pl.BlockSpec((None, n_new, DIL_W), lambda b, kv: (b, 0, 0))] + [cspec(c) for c in caches_t],
        out_shape=[jax.ShapeDtypeStruct((nb, n_new, DIL_W), F32)]
                  + [jax.ShapeDtypeStruct(c.shape, F32) for c in caches_t],
        scratch_shapes=[pltpu.VMEM((rows, c.shape[4]), BF16) for c in caches_t]
                       + [pltpu.VMEM((3, rows, NEW_PAD), F32), pltpu.VMEM((3, 2, rows, 1), F32)],
        input_output_aliases={} if prev is None else {4 + g: 1 + g for g in range(3)},
        compiler_params=_cp(("parallel", "arbitrary"), VMEM_LIMIT),
        name="dil_sample",
    )(qkv_new8, *caches_t, *(() if prev is None else prev))
    return outs[0], tuple(outs[1:])


def _gather_kernel(pt_ref, *refs):
    out_ref = refs[-1]
    page = refs[0].shape[0]
    for p, ref in enumerate(refs[:-1]):
        out_ref[p * page:(p + 1) * page, :] = ref[...]


def gather_pages(cache4, page_table, layer):
    nb, npg = page_table.shape
    page, w = cache4.shape[2], cache4.shape[3]

    def spec(p):
        return pl.BlockSpec((None, None, page, w), lambda b, pt: (layer, pt[b, p], 0, 0))

    grid_spec = pltpu.PrefetchScalarGridSpec(
        num_scalar_prefetch=1, grid=(nb,),
        in_specs=[spec(p) for p in range(npg)],
        out_specs=pl.BlockSpec((npg * page, w), lambda b, pt: (b, 0)))
    return pl.pallas_call(
        _gather_kernel, grid_spec=grid_spec,
        out_shape=jax.ShapeDtypeStruct((nb * npg * page, w), cache4.dtype),
        compiler_params=_cp(("parallel",)),
        name="gather_pages",
    )(page_table, *([cache4] * npg))


def _gelu_tanh(x):
    return x * (0.5 * (1.0 + jnp.tanh(np.sqrt(2.0 / np.pi) * (x + 0.044715 * (x * x * x)))))


def _compress_kernel(xk_ref, xv_ref, pe_ref, w1_ref, w2_ref, k_ref, v_ref, *, n_chunk):
    outs = (k_ref, v_ref)
    for s, x_ref in enumerate((xk_ref, xv_ref)):
        top = jnp.zeros((n_chunk, 2 * CMP_HIDDEN), F32)
        bot = jnp.zeros((n_chunk, 2 * CMP_HIDDEN), F32)
        for l in range(CMP_STRIDE):
            r = x_ref[pl.ds(l, n_chunk, stride=CMP_STRIDE), :]
            top = top + _dot((r + pe_ref[s, l:l + 1, :]).astype(BF16), w1_ref[s, l])
            lb = CMP_STRIDE + l
            bot = bot + _dot((r + pe_ref[s, lb:lb + 1, :]).astype(BF16), w1_ref[s, lb])
        pre = top + pltpu.roll(bot, n_chunk - 1, 0)
        outs[s][...] = _dot(_gelu_tanh(pre).astype(BF16), w2_ref[s])


def compress(x2d, col_blk, n_steps, pe_pair, w1bd, w2bd, *, rows=8192):
    n_chunk = rows // CMP_STRIDE
    out = jax.ShapeDtypeStruct((n_steps * n_chunk, LANES), F32)
    ospec = pl.BlockSpec((n_chunk, LANES), lambda i: (i, 0))
    return pl.pallas_call(
        functools.partial(_compress_kernel, n_chunk=n_chunk),
        grid=(n_steps,),
        in_specs=[pl.BlockSpec((rows, LANES), lambda i: (i, 2 * col_blk)),
                  pl.BlockSpec((rows, LANES), lambda i: (i, 2 * col_blk + 1)),
                  pl.BlockSpec(pe_pair.shape, lambda i: (0, 0, 0)),
                  pl.BlockSpec(w1bd.shape, lambda i: (0, 0, 0, 0)),
                  pl.BlockSpec(w2bd.shape, lambda i: (0, 0, 0))],
        out_specs=[ospec, ospec],
        out_shape=[out, out],
        compiler_params=_cp(("parallel",), VMEM_LIMIT),
        name="compress",
    )(x2d, x2d, pe_pair, w1bd, w2bd)


SEL_TILE = 512
WIN_SPAN = SLD_WINDOW + Q_BLK


def _overlap(n_cmp, n_blk):
    n = _iota((n_cmp, n_blk), 0) * CMP_STRIDE
    j = _iota((n_cmp, n_blk), 1) * SEL_LEN
    return jnp.where(n < j + SEL_LEN, jnp.where(n + CMP_LEN - 1 >= j, 1.0, 0.0), 0.0)


def _block_scores(imp, qpos):
    j = _iota(imp.shape, 1)
    cur = qpos >> 6
    eligible = j <= cur
    forced = (j == 0) | (j == cur) | (j == cur - 1)
    score = jnp.where(eligible, jnp.where(forced, FORCED_SCORE, imp), NEG_INF)
    return score, eligible


def _overlap_t(n_cmp):
    j = _iota((LANES, n_cmp), 0) * SEL_LEN
    n = _iota((LANES, n_cmp), 1) * CMP_STRIDE
    return jnp.where(n < j + SEL_LEN, jnp.where(n + CMP_LEN - 1 >= j, 1.0, 0.0), 0.0)


def _rank_select(sct, eligible_t):
    n_t = LANES // 8
    tiles = [sct[8 * v:8 * v + 8, :] for v in range(n_t)]
    cnts = [jnp.zeros((8, Q_BLK), F32) for _ in range(n_t)]
    sub = _iota((8, Q_BLK), 0)
    for i2 in range(LANES):
        row = jnp.broadcast_to(sct[i2:i2 + 1, :], (8, Q_BLK))
        u = i2 // 8
        for v in range(n_t):
            if u < v:
                ahead = row >= tiles[v]
            elif u > v:
                ahead = row > tiles[v]
            else:
                ahead = (row > tiles[v]) | ((row == tiles[v]) & (sub > i2 % 8))
            cnts[v] = cnts[v] + jnp.where(ahead, 1.0, 0.0)
    cnt = jnp.concatenate(cnts, axis=0)
    return jnp.where(eligible_t, jnp.where(cnt < SEL_TOPK, 1.0, 0.0), 0.0)


def _row_chunks(n):
    return [(a, a + Q_BLK) for a in range(0, n, Q_BLK)]


def _nsa_prompt_kernel(q_ref, kc_ref, vct_ref, ks_ref, vst_ref, kw_ref, vwt_ref, gate_ref, o_ref,
                       qst_ref, sc_ref, p_ref, ps_ref, m_ref, l_ref, al_ref, acc_ref, part_ref):
    i = pl.program_id(1)
    start = i * Q_BLK
    qpos = start + _iota((1, Q_BLK), 1)
    gate_t = (1.0 / (1.0 + jnp.exp(-gate_ref[...]))).T
    n_cmp = kc_ref.shape[0]
    kc = kc_ref[...].astype(BF16)
    vct = vct_ref[...].astype(BF16)
    ov_t = _overlap_t(n_cmp)
    blk_t = _iota((LANES, Q_BLK), 0)
    cur = qpos >> 6
    eligible_t = blk_t <= cur
    forced_t = (blk_t == 0) | (blk_t == cur) | (blk_t == cur - 1)
    ws = pl.multiple_of(jnp.maximum(i - SLD_WINDOW // Q_BLK, 0) * Q_BLK, Q_BLK)
    sub_iota = _iota((Q_BLK, Q_BLK), 0)
    zero = jnp.zeros((Q_BLK, HEAD_DIM), BF16)
    cols = [slice(r * Q_BLK, (r + 1) * Q_BLK) for r in range(NSA_REP)]
    neg_row = jnp.full((1, Q_BLK), NEG_INF, F32)
    n_kt = (start + Q_BLK - 1) // SEL_TILE + 1

    for g in range(2):
        for r in range(NSA_REP):
            h = g * NSA_REP + r
            qh = q_ref[:, h * HEAD_DIM:(h + 1) * HEAD_DIM]
            qst_ref[r * Q_BLK:(r + 1) * Q_BLK, 0:LANES] = jnp.concatenate([qh, zero] if g == 0 else [zero, qh], axis=1)

        sc_ref[0, 0:n_cmp, :] = _nt(kc, qst_ref[:, 0:LANES])
        ps_ref[...] = jnp.zeros_like(ps_ref)
        for r, c in enumerate(cols):
            ok = [((a + sub_iota) * CMP_STRIDE + (CMP_LEN - 1)) <= qpos for a, _ in _row_chunks(n_cmp)]
            m = neg_row
            for k, (a, b) in enumerate(_row_chunks(n_cmp)):
                m = jnp.maximum(m, jnp.max(jnp.where(ok[k], sc_ref[0, a:b, c], NEG_INF), axis=0, keepdims=True))
            den = jnp.zeros((1, Q_BLK), F32)
            for k, (a, b) in enumerate(_row_chunks(n_cmp)):
                e = jnp.where(ok[k], jnp.exp2(sc_ref[0, a:b, c] - m), 0.0)
                den = den + jnp.sum(e, axis=0, keepdims=True)
                sc_ref[0, a:b, c] = e
            inv = 1.0 / jnp.where(den > 0, den, 1.0)
            for a, b in _row_chunks(n_cmp):
                p = sc_ref[0, a:b, c] * inv
                ps_ref[a:b, :] = ps_ref[a:b, :] + p
                p_ref[0, a:b, c] = p.astype(BF16)
        oc_t = _dot(vct, p_ref[0, 0:n_cmp, :])
        for r, c in enumerate(cols):
            h = g * NSA_REP + r
            part_ref[:, c] = gate_t[h:h + 1, :] * oc_t[:, c]

        imp_t = jnp.dot(ov_t, ps_ref[...], preferred_element_type=F32, precision=lax.Precision.HIGHEST)
        sct = jnp.where(eligible_t, jnp.where(forced_t, FORCED_SCORE, imp_t), NEG_INF)
        sel_t = _rank_select(sct, eligible_t)
        sel_far = jnp.where((blk_t >> 1) == i, 0.0, sel_t)
        sel_bias = jnp.where(sel_far.T > 0.5, 0.0, NEG_INF).astype(BF16)
        for r in range(NSA_REP):
            qst_ref[r * Q_BLK:(r + 1) * Q_BLK, LANES:2 * LANES] = sel_bias

        own = pl.multiple_of(start, Q_BLK)
        sc_ref[0, 0:Q_BLK, :] = _nt(ks_ref[pl.ds(own, Q_BLK), :], qst_ref[:, 0:LANES])
        tri = sub_iota <= _iota((Q_BLK, Q_BLK), 1)
        for c in cols:
            s_own = jnp.where(tri, sc_ref[0, 0:Q_BLK, c], NEG_INF)
            m_own = jnp.max(s_own, axis=0, keepdims=True)
            p = jnp.exp2(s_own - m_own)
            m_ref[:, c] = m_own
            l_ref[:, c] = jnp.sum(p, axis=0, keepdims=True)
            p_ref[0, 0:Q_BLK, c] = p.astype(BF16)
        acc_ref[...] = _dot(vst_ref[:, pl.ds(own, Q_BLK)], p_ref[0, 0:Q_BLK, :])

        def sel_tile(kt, slot):
            k0 = pl.multiple_of(kt * SEL_TILE, SEL_TILE)
            ks = ks_ref[pl.ds(k0, SEL_TILE), :]
            vst = vst_ref[:, pl.ds(k0, SEL_TILE)]
            blk = kt * (SEL_TILE // SEL_LEN) + (_iota((SEL_TILE, LANES), 0) >> 6)
            onehot = jnp.where(_iota((SEL_TILE, LANES), 1) == blk, 1.0, 0.0).astype(BF16)
            kaug = jnp.concatenate([ks, onehot], axis=1)
            sc_ref[slot, 0:SEL_TILE, :] = _nt(kaug, qst_ref[...])
            for c in cols:
                m_old = m_ref[:, c]
                m_new = m_old
                for a, b in _row_chunks(SEL_TILE):
                    m_new = jnp.maximum(m_new, jnp.max(sc_ref[slot, a:b, c], axis=0, keepdims=True))
                alpha = jnp.exp2(m_old - m_new)
                ls = jnp.zeros((1, Q_BLK), F32)
                for a, b in _row_chunks(SEL_TILE):
                    p = jnp.exp2(sc_ref[slot, a:b, c] - m_new)
                    ls = ls + jnp.sum(p, axis=0, keepdims=True)
                    p_ref[slot, a:b, c] = p.astype(BF16)
                l_ref[:, c] = alpha * l_ref[:, c] + ls
                m_ref[:, c] = m_new
                al_ref[:, c] = alpha
            acc_ref[...] = al_ref[...] * acc_ref[...] + _dot(vst, p_ref[slot, 0:SEL_TILE, :])

        def pair_body(j, carry):
            sel_tile(2 * j, 0)
            sel_tile(2 * j + 1, 1)
            return carry

        def single_body(kt, carry):
            sel_tile(kt, 0)
            return carry

        lax.fori_loop(0, n_kt // 2, pair_body, 0)
        lax.fori_loop(2 * (n_kt // 2), n_kt, single_body, 0)

        sc_ref[0, 0:WIN_SPAN, :] = _nt(kw_ref[pl.ds(ws, WIN_SPAN), :], qst_ref[:, 0:LANES])
        for c in cols:
            ok = []
            for a, _ in _row_chunks(WIN_SPAN):
                dist = qpos - (ws + a + sub_iota)
                ok.append((dist >= 0) & (dist <= SLD_WINDOW))
            m = neg_row
            for k, (a, b) in enumerate(_row_chunks(WIN_SPAN)):
                m = jnp.maximum(m, jnp.max(jnp.where(ok[k], sc_ref[0, a:b, c], NEG_INF), axis=0, keepdims=True))
            den = jnp.zeros((1, Q_BLK), F32)
            for k, (a, b) in enumerate(_row_chunks(WIN_SPAN)):
                e = jnp.where(ok[k], jnp.exp2(sc_ref[0, a:b, c] - m), 0.0)
                den = den + jnp.sum(e, axis=0, keepdims=True)
                p_ref[0, a:b, c] = e.astype(BF16)
            al_ref[:, c] = 1.0 / den
        ow_t = _dot(vwt_ref[:, pl.ds(ws, WIN_SPAN)], p_ref[0, 0:WIN_SPAN, :]) * al_ref[...]

        for r, c in enumerate(cols):
            h = g * NSA_REP + r
            tot_t = (part_ref[:, c]
                     + gate_t[NSA_HEADS + h:NSA_HEADS + h + 1, :] * (acc_ref[:, c] / l_ref[:, c])
                     + gate_t[2 * NSA_HEADS + h:2 * NSA_HEADS + h + 1, :] * ow_t[:, c])
            tot = tot_t.T
            o_ref[:, h * HEAD_DIM:(h + 1) * HEAD_DIM] = tot[:, g * HEAD_DIM:(g + 1) * HEAD_DIM].astype(BF16)


def nsa_prompt(proj16, proj32, kcmp, vcmp_t, vslc_t, vwin_t, *, bsz, seq):
    n_i = seq // Q_BLK
    n_cmp = seq // CMP_STRIDE
    assert seq >= WIN_SPAN and seq % SEL_TILE == 0 and seq // SEL_LEN <= LANES and n_cmp % Q_BLK == 0
    kspec = lambda col: pl.BlockSpec((seq, LANES), lambda b, i: (b, col // LANES))
    vtspec = pl.BlockSpec((LANES, seq), lambda b, i: (b, 0))
    wide = NSA_REP * Q_BLK
    n_sc = max(n_cmp, SEL_TILE, WIN_SPAN)
    return pl.pallas_call(
        _nsa_prompt_kernel,
        grid=(bsz, n_i),
        in_specs=[pl.BlockSpec((Q_BLK, B_Q), lambda b, i: (b * n_i + i, 0)),
                  pl.BlockSpec((n_cmp, LANES), lambda b, i: (b, 0)),
                  pl.BlockSpec((LANES, n_cmp), lambda b, i: (b, 0)),
                  kspec(B_KSLC), vtspec, kspec(B_KWIN), vtspec,
                  pl.BlockSpec((Q_BLK, LANES), lambda b, i: (b * n_i + i, B_GATE // LANES))],
        out_specs=pl.BlockSpec((Q_BLK, B_Q), lambda b, i: (b * n_i + i, 0)),
        out_shape=jax.ShapeDtypeStruct((bsz * seq, B_Q), BF16),
        scratch_shapes=[pltpu.VMEM((wide, 2 * LANES), BF16),
                        pltpu.VMEM((2, n_sc, wide), F32),
                        pltpu.VMEM((2, n_sc, wide), BF16),
                        pltpu.VMEM((n_cmp, Q_BLK), F32),
                        pltpu.VMEM((1, wide), F32),
                        pltpu.VMEM((1, wide), F32),
                        pltpu.VMEM((1, wide), F32),
                        pltpu.VMEM((LANES, wide), F32),
                        pltpu.VMEM((LANES, wide), F32)],
        compiler_params=_cp(("parallel", "arbitrary"), VMEM_LIMIT),
        name="nsa_prompt",
    )(proj16, kcmp, vcmp_t, proj16, vslc_t, proj16, vwin_t, proj32)


def _nsa_sample_kernel(qpad_ref, kc_ref, vc_ref, past_ref, new_ref, win_ref, gate_ref, o_ref,
                       *, n_new, n_past):
    rows = qpad_ref.shape[0]
    rho = _iota((rows, 1), 0)
    t_row = (rho >> 3) % n_new
    qpos = n_past + t_row
    qpad = qpad_ref[...]
    gate = 1.0 / (1.0 + jnp.exp(-gate_ref[...]))
    n_pad = new_ref.shape[0]
    tnew = _iota((rows, n_pad), 1)
    new = new_ref[...]
    new16 = new.astype(BF16)
    new_r = new16.astype(F32)

    def attend(s_past, ok_past, v_past16, s_new, ok_new, v_new_r):
        s_past = jnp.where(ok_past, s_past, NEG_INF)
        s_new = jnp.where(ok_new, s_new, NEG_INF)
        m = jnp.maximum(jnp.max(s_past, axis=-1, keepdims=True), jnp.max(s_new, axis=-1, keepdims=True))
        e_past = jnp.where(ok_past, jnp.exp2(s_past - m), 0.0)
        e_new = jnp.where(ok_new, jnp.exp2(s_new - m), 0.0)
        den = jnp.sum(e_past, axis=-1, keepdims=True) + jnp.sum(e_new, axis=-1, keepdims=True)
        o = _dot(e_past.astype(BF16), v_past16)
        e_new_r = e_new.astype(BF16).astype(F32)
        for t in range(n_new):
            o = o + e_new_r[:, t:t + 1] * v_new_r[t:t + 1, :]
        return o / den

    n_cmp = kc_ref.shape[0]
    ok_c = (_iota((rows, n_cmp), 1) * CMP_STRIDE + (CMP_LEN - 1)) <= qpos
    s = jnp.where(ok_c, _nt(qpad, kc_ref[...].astype(BF16)), NEG_INF)
    m = jnp.max(s, axis=-1, keepdims=True)
    e = jnp.where(ok_c, jnp.exp2(s - m), 0.0)
    den = jnp.sum(e, axis=-1, keepdims=True)
    p = e / jnp.where(den > 0, den, 1.0)
    o_cmp = _dot(p.astype(BF16), vc_ref[...].astype(BF16))

    same = jnp.where((_iota((rows, rows), 0) >> 3) == (_iota((rows, rows), 1) >> 3), 1.0, 0.0)
    psum = jnp.dot(same, p, preferred_element_type=F32, precision=lax.Precision.HIGHEST)
    imp = jnp.dot(psum, _overlap(n_cmp, LANES), preferred_element_type=F32, precision=lax.Precision.HIGHEST)
    score, eligible = _block_scores(imp, qpos)
    n_sel = -(-(n_past + n_new) // SEL_LEN)
    jcol = _iota(score.shape, 1)
    cnt = jnp.zeros(score.shape, F32)
    for i2 in range(n_sel):
        col = score[:, i2:i2 + 1]
        cnt = cnt + jnp.where(col > score, 1.0, jnp.where(col == score, jnp.where(jcol > i2, 1.0, 0.0), 0.0))
    sel = jnp.where(eligible, jnp.where(cnt < SEL_TOPK, 1.0, 0.0), 0.0)

    expand = jnp.where(_iota((LANES, n_past), 0) == (_iota((LANES, n_past), 1) >> 6), 1.0, 0.0).astype(BF16)
    selk = _dot(sel.astype(BF16), expand)
    past16 = past_ref[...].astype(BF16)
    blk_new = n_past // SEL_LEN
    ok_new_sel = (sel[:, blk_new:blk_new + 1] > 0.5) & (tnew <= t_row)
    o_sel = attend(_nt(qpad, past16[:, :LANES]), selk > 0.5, past16[:, LANES:],
                   _nt(qpad, new16[:, :LANES]), ok_new_sel, new_r[:, LANES:2 * LANES])

    win16 = win_ref[...].astype(BF16)
    n_win = win16.shape[0]
    ok_w = _iota((rows, n_win), 1) >= t_row + (n_win - SLD_WINDOW)
    o_win = attend(_nt(qpad, win16[:, :LANES]), ok_w, win16[:, LANES:],
                   _nt(qpad, new16[:, 2 * LANES:3 * LANES]), tnew <= t_row, new_r[:, 3 * LANES:])

    o = gate[:, 0:1] * o_cmp + gate[:, 1:2] * o_sel + gate[:, 2:3] * o_win
    o_ref[...] = jnp.where(rho < rows // 2, o[:, :HEAD_DIM], o[:, HEAD_DIM:])


def nsa_sample(qpad, kcmp, vcmp, past2d, newkv, win_cache, gates, layer, *, n_new, n_past):
    nb, rows, _ = qpad.shape
    n_cmp = n_past // CMP_STRIDE
    n_win = win_cache.shape[2]
    assert n_past % SEL_LEN == 0 and n_new <= SEL_LEN and n_win >= SLD_WINDOW
    return pl.pallas_call(
        functools.partial(_nsa_sample_kernel, n_new=n_new, n_past=n_past),
        grid=(nb,),
        in_specs=[pl.BlockSpec((None, rows, LANES), lambda b: (b, 0, 0)),
                  pl.BlockSpec((n_cmp, LANES), lambda b: (b, 0)),
                  pl.BlockSpec((n_cmp, LANES), lambda b: (b, 0)),
                  pl.BlockSpec((n_past, 2 * LANES), lambda b: (b, 1)),
                  pl.BlockSpec((None,) + newkv.shape[1:], lambda b: (b, 0, 0)),
                  pl.BlockSpec((None, None, n_win, 2 * LANES), lambda b: (layer, b, 0, 0)),
                  pl.BlockSpec((None, rows, LANES), lambda b: (b, 0, 0))],
        out_specs=pl.BlockSpec((None, rows, HEAD_DIM), lambda b: (b, 0, 0)),
        out_shape=jax.ShapeDtypeStruct((nb, rows, HEAD_DIM), F32),
        compiler_params=_cp(("parallel",), VMEM_LIMIT),
        name="nsa_sample",
    )(qpad, kcmp, vcmp, past2d, newkv, win_cache, gates)


def _rope_tables(pos):
    inv = ROPE_THETA ** (-jnp.arange(HALF, dtype=F32) / HALF)
    ang = pos.astype(F32)[:, None] * inv[None, :]
    cos, sin, zero = jnp.cos(ang), jnp.sin(ang), jnp.zeros_like(ang)
    return (jnp.concatenate([cos, cos] * 2, axis=1),
            jnp.concatenate([-sin, zero] * 2, axis=1),
            jnp.concatenate([zero, sin] * 2, axis=1))


def _column_roles_a():
    slab = (np.arange(3 * 3 * DIL_W) % (3 * DIL_W)) // DIL_W
    rope = (slab < 2).astype(np.float32)
    scale = np.where(slab == 0, ATTN_SCALE, 1.0).astype(np.float32)
    return rope[None, :], scale[None, :]


def _column_roles_b():
    col = np.arange(B_COLS)
    is_q = col < B_Q
    is_k = (col >= B_KCMP) & (col < B_GATE) & ((col - B_KCMP) % (2 * LANES) < LANES)
    rope = (is_q | is_k).astype(np.float32)
    scale = np.where(is_q, ATTN_SCALE * LOG2E, 1.0).astype(np.float32)
    return rope[None, :], scale[None, :]


def _block_diag2(w):
    z = jnp.zeros_like(w)
    return jnp.concatenate([jnp.concatenate([w, z], axis=-1), jnp.concatenate([z, w], axis=-1)], axis=-2)


def kernel(x_prompt, x_sample, p_prompt, p_sample, cache_a_w128, cache_a_w512, cache_a_w2048, cache_b_kv, cache_b_win, page_table, norm_mix, norm_ffn, norm_ple, norm_final, a_w_in, a_w_out, b_w_in, b_w_out, b_cmp_pe, b_cmp_w1, b_cmp_w2, moe_w_group, moe_b_group, moe_w_expert, moe_b_expert, moe_w_gate, moe_w_up, moe_w_down, ple_w_proj, ple_w_gate):
    bsz, seq, d = x_prompt.shape
    nb, n_new, _ = x_sample.shape
    depth = norm_mix.shape[0]
    n_pages, page = page_table.shape[1], cache_b_kv.shape[2]
    n_past = n_pages * page
    np_ = bsz * seq
    ns = nb * n_new

    h = jnp.concatenate([x_prompt.reshape(np_, d), x_sample.reshape(ns, d)], axis=0)
    pos = jnp.concatenate([jnp.tile(jnp.arange(seq, dtype=jnp.int32), bsz),
                           jnp.tile(n_past + jnp.arange(n_new, dtype=jnp.int32), nb)])
    rope = _rope_tables(pos)
    roles_a = tuple(jnp.asarray(a) for a in _column_roles_a())
    roles_b = tuple(jnp.asarray(a) for a in _column_roles_b())

    caches_t = [jnp.transpose(c, (0, 1, 3, 4, 5, 2)).reshape(c.shape[0], nb, 2, DIL_W, c.shape[2])
                for c in (cache_a_w128, cache_a_w512, cache_a_w2048)]
    a_upd = None
    kv_w = cache_b_kv.shape[3] * cache_b_kv.shape[4] * cache_b_kv.shape[5]
    cache_kv4 = cache_b_kv.reshape(cache_b_kv.shape[0], cache_b_kv.shape[1], page, kv_w)
    cache_win4 = cache_b_win.reshape(cache_b_win.shape[0], nb, cache_b_win.shape[2], 2 * LANES)

    a_bufs_p, b_rows_p, b_rows_s, b_win_p, b_win_s = [], [], [], [], []
    for i in range(depth):
        li = i // 2
        if i % 2 == 0:
            qkv32, qkv16 = norm_proj(h, norm_mix[i], a_w_in[li].astype(BF16), *roles_a, *rope, tn=3 * DIL_W)
            parts = [dil_prompt(qkv16, g, bsz=bsz, seq=seq) for g in range(3)]
            o_p = dil_merge([p[0] for p in parts], [p[1] for p in parts])
            qkv_new8 = jnp.pad(qkv32[np_:].reshape(nb, n_new, qkv32.shape[1]), ((0, 0), (NEW_PAD - n_new, 0), (0, 0)))
            o_s, a_upd = dil_sample(qkv_new8, caches_t, a_upd, li, n_new=n_new)
            o16 = jnp.concatenate([o_p, o_s.reshape(ns, DIL_W).astype(BF16)], axis=0)
            h = out_proj(h, o16, a_w_out[li].astype(BF16))
            bp = []
            for g in range(3):
                lo, hi = g * 3 * DIL_W + DIL_W, (g + 1) * 3 * DIL_W
                keep = min(DIL_WINDOWS[g], seq)
                tail = jnp.stack([qkv32[(b + 1) * seq - keep:(b + 1) * seq, lo:hi] for b in range(bsz)], axis=0)
                bp.append(tail.reshape(bsz, keep, 2, DIL_HEADS, HEAD_DIM))
            a_bufs_p.append(bp)
        else:
            w_in = jnp.pad(b_w_in[li], ((0, 0), (0, B_COLS - b_w_in.shape[2]))).astype(BF16)
            p32, p16 = norm_proj(h, norm_mix[i], w_in, *roles_b, *rope, tn=B_COLS // 2)
            pe_pair = jnp.concatenate([b_cmp_pe[li], b_cmp_pe[li]], axis=-1)
            w1bd = _block_diag2(b_cmp_w1[li].reshape(2, CMP_LEN, HEAD_DIM, CMP_HIDDEN)).astype(BF16)
            w2bd = _block_diag2(b_cmp_w2[li]).astype(BF16)
            kc_p, vc_p = compress(p32, B_KCMP // (2 * LANES), bsz, pe_pair, w1bd, w2bd, rows=seq)
            n_cmp = seq // CMP_STRIDE
            vc_t = vc_p.reshape(bsz, n_cmp, LANES).transpose(0, 2, 1).reshape(bsz * LANES, n_cmp)
            vs_t = p16[:np_, B_VSLC:B_VSLC + LANES].reshape(bsz, seq, LANES).transpose(0, 2, 1).reshape(bsz * LANES, seq)
            vw_t = p16[:np_, B_VWIN:B_VWIN + LANES].reshape(bsz, seq, LANES).transpose(0, 2, 1).reshape(bsz * LANES, seq)
            o_p = nsa_prompt(p16, p32, kc_p, vc_t, vs_t, vw_t, bsz=bsz, seq=seq)
            past = gather_pages(cache_kv4, page_table, li)
            step_rows = max(n_past, 8192)
            kc_s, vc_s = compress(past, 0, nb * n_past // step_rows, pe_pair, w1bd, w2bd, rows=step_rows)
            ps32, ps16 = p32[np_:], p16[np_:]
            q = ps16[:, :B_Q].reshape(nb, n_new, 2, NSA_REP, HEAD_DIM).transpose(0, 2, 1, 3, 4)
            zq = jnp.zeros_like(q[:, 0])
            qpad = jnp.stack([jnp.concatenate([q[:, 0], zq], axis=-1),
                              jnp.concatenate([zq, q[:, 1]], axis=-1)], axis=1)
            qpad = qpad.reshape(nb, 2 * n_new * NSA_REP, LANES)
            gts = ps32[:, B_GATE:B_GATE + 3 * NSA_HEADS].reshape(nb, n_new, 3, 2, NSA_REP)
            gts = gts.transpose(0, 3, 1, 4, 2).reshape(nb, 2 * n_new * NSA_REP, 3)
            gts = jnp.pad(gts, ((0, 0), (0, 0), (0, LANES - 3)))
            newkv = ps32[:, B_KSLC:B_GATE].reshape(nb, n_new, B_GATE - B_KSLC)
            newkv = jnp.pad(newkv, ((0, 0), (0, 8 - n_new), (0, 0)))
            o_s = nsa_sample(qpad, kc_s, vc_s, past, newkv, cache_win4, gts, li, n_new=n_new, n_past=n_past)
            o_s = o_s.reshape(nb, 2, n_new, NSA_REP, HEAD_DIM).transpose(0, 2, 1, 3, 4).reshape(ns, B_Q)
            o16 = jnp.concatenate([o_p, o_s.astype(BF16)], axis=0)
            h = out_proj(h, o16, b_w_out[li].astype(BF16))
            keep = min(SLD_WINDOW, seq)
            b_rows_p.append(p32[:np_, B_KCMP:B_KWIN].reshape(bsz, seq, 4, 2, HEAD_DIM))
            win_tail = jnp.stack([p32[(b + 1) * seq - keep:(b + 1) * seq, B_KWIN:B_GATE] for b in range(bsz)], axis=0)
            b_win_p.append(win_tail.reshape(bsz, keep, 2, 2, HEAD_DIM))
            b_rows_s.append(ps32[:, B_KCMP:B_KWIN].reshape(nb, n_new, 4, 2, HEAD_DIM))
            new_win = ps32[:, B_KWIN:B_GATE].reshape(nb, n_new, 2, 2, HEAD_DIM)
            b_win_s.append(jnp.concatenate([cache_b_win[li][:, n_new:], new_win], axis=1))

        w_route = jnp.pad(jnp.concatenate([moe_w_group[i], moe_w_expert[i]], axis=1),
                          ((0, 0), (0, LANES - N_GROUPS - N_EXPERTS)))
        b_route = jnp.pad(jnp.concatenate([moe_b_group[i], moe_b_expert[i]]),
                          (0, LANES - N_GROUPS - N_EXPERTS)).reshape(1, LANES)
        h = moe(h, norm_ffn[i], w_route, b_route, moe_w_gate[i].astype(BF16),
                moe_w_up[i].astype(BF16), moe_w_down[i].astype(BF16))
        p_tok = jnp.concatenate([p_prompt[i].reshape(np_, -1), p_sample[i].reshape(ns, -1)], axis=0)
        h = ple(h, norm_ple[i], p_tok, ple_w_gate[i].astype(BF16), ple_w_proj[i].astype(BF16))

    y = rms_norm(h, norm_final)
    a_state = [jnp.transpose(u.reshape(u.shape[0], nb, 2, DIL_HEADS, HEAD_DIM, u.shape[4]), (0, 1, 5, 2, 3, 4))
               for u in a_upd]
    stack = lambda bufs, g: jnp.stack([b[g] for b in bufs], axis=0)
    return (y[:np_].reshape(bsz, seq, d), y[np_:].reshape(nb, n_new, d),
            stack(a_bufs_p, 0), stack(a_bufs_p, 1), stack(a_bufs_p, 2),
            jnp.stack(b_rows_p, axis=0), jnp.stack(b_win_p, axis=0),
            *a_state,
            jnp.stack(b_rows_s, axis=0), jnp.stack(b_win_s, axis=0))
```

```python
import functools

import numpy as np
import jax
import jax.numpy as jnp
from jax import lax
from jax.experimental import pallas as pl
from jax.experimental.pallas import tpu as pltpu

F32 = jnp.float32
BF16 = jnp.bfloat16

HEAD_DIM = 64
HALF = HEAD_DIM // 2
ROPE_THETA = 10000.0
RMS_EPS = 1e-6
NEG_INF = -1e30
ATTN_SCALE = HEAD_DIM ** -0.5
LOG2E = 1.4426950408889634
LANES = 128
Q_BLK = 128

DIL_WINDOWS = (128, 512, 2048)
DIL_RATES = (1, 4, 16)
DIL_HEADS = 8
DIL_W = DIL_HEADS * HEAD_DIM

NSA_HEADS = 16
NSA_REP = 8
CMP_LEN = 32
CMP_STRIDE = 16
CMP_HIDDEN = 256
SEL_LEN = 64
SEL_TOPK = 16
SLD_WINDOW = 512
FORCED_SCORE = 1e9
B_COLS = 2048
B_Q = 1024
B_KCMP, B_VCMP, B_KSLC, B_VSLC, B_KWIN, B_VWIN, B_GATE = (1024, 1152, 1280, 1408, 1536, 1664, 1792)

N_GROUPS = 4
EXP_PER_GROUP = 8
N_EXPERTS = 32

VMEM_LIMIT = 56 * 1024 * 1024


def _cp(sem, vmem=None):
    return pltpu.CompilerParams(dimension_semantics=sem, vmem_limit_bytes=vmem)


def _nt(a, b):
    return lax.dot_general(a, b, (((1,), (1,)), ((), ())), preferred_element_type=F32)


def _dot(a, b):
    return jnp.dot(a, b, preferred_element_type=F32)


def _iota(shape, dim):
    return lax.broadcasted_iota(jnp.int32, shape, dim)


def _norm_proj_kernel(x_ref, g_ref, w_ref, rc_ref, sc_ref, c_ref, s1_ref, s2_ref,
                      o32_ref, o16_ref, xn_ref, *, tn):
    @pl.when(pl.program_id(1) == 0)
    def _():
        x = x_ref[...]
        var = jnp.mean(x * x, axis=-1, keepdims=True)
        xn_ref[...] = (x * lax.rsqrt(var + RMS_EPS) * g_ref[...]).astype(BF16)

    y = _dot(xn_ref[...], w_ref[...])
    reps = tn // LANES
    rc = rc_ref[...]
    c = 1.0 + rc * (jnp.tile(c_ref[...], (1, reps)) - 1.0)
    s1 = rc * jnp.tile(s1_ref[...], (1, reps))
    s2 = rc * jnp.tile(s2_ref[...], (1, reps))
    y = y * c + pltpu.roll(y, tn - HALF, 1) * s1 + pltpu.roll(y, HALF, 1) * s2
    o32_ref[...] = y
    o16_ref[...] = (y * sc_ref[...]).astype(BF16)


def norm_proj(x, g, w16, ropecol, colscale, rope_c, rope_s1, rope_s2, *, tm=512, tn=512):
    n, d = x.shape
    ncols = w16.shape[1]
    assert n % tm == 0 and ncols % tn == 0
    return pl.pallas_call(
        functools.partial(_norm_proj_kernel, tn=tn),
        grid=(n // tm, ncols // tn),
        in_specs=[
            pl.BlockSpec((tm, d), lambda i, j: (i, 0)),
            pl.BlockSpec((1, d), lambda i, j: (0, 0)),
            pl.BlockSpec((d, tn), lambda i, j: (0, j)),
            pl.BlockSpec((1, tn), lambda i, j: (0, j)),
            pl.BlockSpec((1, tn), lambda i, j: (0, j)),
            pl.BlockSpec((tm, LANES), lambda i, j: (i, 0)),
            pl.BlockSpec((tm, LANES), lambda i, j: (i, 0)),
            pl.BlockSpec((tm, LANES), lambda i, j: (i, 0)),
        ],
        out_specs=[pl.BlockSpec((tm, tn), lambda i, j: (i, j)),
                   pl.BlockSpec((tm, tn), lambda i, j: (i, j))],
        out_shape=[jax.ShapeDtypeStruct((n, ncols), F32),
                   jax.ShapeDtypeStruct((n, ncols), BF16)],
        scratch_shapes=[pltpu.VMEM((tm, d), BF16)],
        compiler_params=_cp(("parallel", "arbitrary")),
        name="norm_proj",
    )(x, g.reshape(1, d), w16, ropecol, colscale, rope_c, rope_s1, rope_s2)


def _out_proj_kernel(h_ref, o_ref, w_ref, out_ref):
    out_ref[...] = h_ref[...] + _dot(o_ref[...], w_ref[...])


def out_proj(h, o16, w16, *, tm=512):
    n, d = h.shape
    k = o16.shape[1]
    return pl.pallas_call(
        _out_proj_kernel,
        grid=(n // tm,),
        in_specs=[pl.BlockSpec((tm, d), lambda i: (i, 0)),
                  pl.BlockSpec((tm, k), lambda i: (i, 0)),
                  pl.BlockSpec((k, d), lambda i: (0, 0))],
        out_specs=pl.BlockSpec((tm, d), lambda i: (i, 0)),
        out_shape=jax.ShapeDtypeStruct((n, d), F32),
        compiler_params=_cp(("parallel",)),
        name="out_proj",
    )(h, o16, w16)


def _moe_kernel(h_ref, g_ref, wr_ref, br_ref, wg_ref, wu_ref, wd_ref, out_ref,
                xn_ref, comb_ref, acc_ref):
    grp = pl.program_id(1)

    @pl.when(grp == 0)
    def _():
        x = h_ref[...]
        var = jnp.mean(x * x, axis=-1, keepdims=True)
        xn = x * lax.rsqrt(var + RMS_EPS) * g_ref[...]
        xn_ref[...] = xn.astype(BF16)
        logit = jnp.dot(xn, wr_ref[...], preferred_element_type=F32,
                        precision=lax.Precision.HIGHEST) + br_ref[...]
        lane = _iota(logit.shape, 1)
        is_grp = lane < N_GROUPS
        lg = jnp.where(is_grp, logit, -jnp.inf)
        gmax = jnp.max(lg, axis=-1, keepdims=True)
        gidx = jnp.min(jnp.where(lg == gmax, lane, LANES), axis=-1, keepdims=True)
        p_grp = 1.0 / jnp.sum(jnp.where(is_grp, jnp.exp(logit - gmax), 0.0), axis=-1, keepdims=True)
        lo = N_GROUPS + gidx * EXP_PER_GROUP
        in_grp = (lane >= lo) & (lane < lo + EXP_PER_GROUP)
        le = jnp.where(in_grp, logit, -jnp.inf)
        v1 = jnp.max(le, axis=-1, keepdims=True)
        i1 = jnp.min(jnp.where(le == v1, lane, LANES), axis=-1, keepdims=True)
        le2 = jnp.where(lane == i1, -jnp.inf, le)
        v2 = jnp.max(le2, axis=-1, keepdims=True)
        i2 = jnp.min(jnp.where(le2 == v2, lane, LANES), axis=-1, keepdims=True)
        e2 = jnp.exp(v2 - v1)
        w1 = p_grp / (1.0 + e2)
        w2 = p_grp * e2 / (1.0 + e2)
        comb = jnp.where(lane == i1, w1, jnp.where(lane == i2, w2, 0.0))
        for gg in range(N_GROUPS):
            sel = jnp.where(gidx == gg, comb, 0.0)
            comb_ref[gg] = pltpu.roll(sel, LANES - (N_GROUPS + gg * EXP_PER_GROUP), 1)
        acc_ref[...] = jnp.zeros_like(acc_ref)

    xn = xn_ref[...]
    comb = comb_ref[grp]
    acc = acc_ref[...]
    for k in range(EXP_PER_GROUP):
        hg = _dot(xn, wg_ref[k])
        hu = _dot(xn, wu_ref[k])
        a = (hg / (1.0 + jnp.exp(-hg))) * hu * comb[:, k:k + 1]
        acc = acc + _dot(a.astype(BF16), wd_ref[k])
    acc_ref[...] = acc

    @pl.when(grp == N_GROUPS - 1)
    def _():
        out_ref[...] = h_ref[...] + acc_ref[...]


def moe(h, g, w_route, b_route, wg16, wu16, wd16, *, tm=512):
    n, d = h.shape
    f = wg16.shape[2]
    return pl.pallas_call(
        _moe_kernel,
        grid=(n // tm, N_GROUPS),
        in_specs=[
            pl.BlockSpec((tm, d), lambda i, e: (i, 0)),
            pl.BlockSpec((1, d), lambda i, e: (0, 0)),
            pl.BlockSpec((d, LANES), lambda i, e: (0, 0)),
            pl.BlockSpec((1, LANES), lambda i, e: (0, 0)),
            pl.BlockSpec((EXP_PER_GROUP, d, f), lambda i, e: (e, 0, 0)),
            pl.BlockSpec((EXP_PER_GROUP, d, f), lambda i, e: (e, 0, 0)),
            pl.BlockSpec((EXP_PER_GROUP, f, d), lambda i, e: (e, 0, 0)),
        ],
        out_specs=pl.BlockSpec((tm, d), lambda i, e: (i, 0)),
        out_shape=jax.ShapeDtypeStruct((n, d), F32),
        scratch_shapes=[pltpu.VMEM((tm, d), BF16),
                        pltpu.VMEM((N_GROUPS, tm, LANES), F32),
                        pltpu.VMEM((tm, d), F32)],
        compiler_params=_cp(("parallel", "arbitrary"), VMEM_LIMIT),
        name="moe",
    )(h, g.reshape(1, d), w_route, b_route, wg16, wu16, wd16)


def _ple_kernel(h_ref, g_ref, p_ref, wgate_ref, wproj_ref, out_ref):
    x = h_ref[...]
    var = jnp.mean(x * x, axis=-1, keepdims=True)
    xn = (x * lax.rsqrt(var + RMS_EPS) * g_ref[...]).astype(BF16)
    gate = 1.0 / (1.0 + jnp.exp(-_dot(xn, wgate_ref[...])))
    out_ref[...] = x + gate * _dot(p_ref[...].astype(BF16), wproj_ref[...])


def ple(h, g, p, wgate16, wproj16, *, tm=512):
    n, d = h.shape
    pd = p.shape[1]
    return pl.pallas_call(
        _ple_kernel,
        grid=(n // tm,),
        in_specs=[pl.BlockSpec((tm, d), lambda i: (i, 0)),
                  pl.BlockSpec((1, d), lambda i: (0, 0)),
                  pl.BlockSpec((tm, pd), lambda i: (i, 0)),
                  pl.BlockSpec((d, d), lambda i: (0, 0)),
                  pl.BlockSpec((pd, d), lambda i: (0, 0))],
        out_specs=pl.BlockSpec((tm, d), lambda i: (i, 0)),
        out_shape=jax.ShapeDtypeStruct((n, d), F32),
        compiler_params=_cp(("parallel",)),
        name="ple",
    )(h, g.reshape(1, d), p, wgate16, wproj16)


def _rms_kernel(h_ref, g_ref, out_ref):
    x = h_ref[...]
    var = jnp.mean(x * x, axis=-1, keepdims=True)
    out_ref[...] = x * lax.rsqrt(var + RMS_EPS) * g_ref[...]


def rms_norm(h, g, *, tm=512):
    n, d = h.shape
    return pl.pallas_call(
        _rms_kernel,
        grid=(n // tm,),
        in_specs=[pl.BlockSpec((tm, d), lambda i: (i, 0)),
                  pl.BlockSpec((1, d), lambda i: (0, 0))],
        out_specs=pl.BlockSpec((tm, d), lambda i: (i, 0)),
        out_shape=jax.ShapeDtypeStruct((n, d), F32),
        compiler_params=_cp(("parallel",)),
        name="final_norm",
    )(h, g.reshape(1, d))


def _dil_prompt_kernel(q_ref, kp_ref, kc_ref, vp_ref, vc_ref, o_ref, lse_ref):
    i = pl.program_id(2)
    q = q_ref[...]
    kk = jnp.concatenate([kp_ref[...], kc_ref[...]], axis=0)
    vv = jnp.concatenate([vp_ref[...], vc_ref[...]], axis=0)
    qu = i * Q_BLK + _iota((Q_BLK, 2 * Q_BLK), 0)
    ku = (i - 1) * Q_BLK + _iota((Q_BLK, 2 * Q_BLK), 1)
    dist = qu - ku
    valid = (ku >= 0) & (dist >= 0) & (dist <= Q_BLK)
    for h in range(DIL_HEADS):
        sl = slice(h * HEAD_DIM, (h + 1) * HEAD_DIM)
        s = jnp.where(valid, _nt(q[:, sl], kk[:, sl]), NEG_INF)
        m = jnp.max(s, axis=-1, keepdims=True)
        e = jnp.exp(s - m)
        den = jnp.sum(e, axis=-1, keepdims=True)
        o = _dot(e.astype(BF16), vv[:, sl]) / den
        o_ref[:, sl] = o
        lse_ref[:, sl] = jnp.broadcast_to(m + jnp.log(den), (Q_BLK, HEAD_DIM))


def dil_prompt(qkv16, g, *, bsz, seq):
    r = DIL_RATES[g]
    if r == 1:
        view, nslab, first = qkv16, qkv16.shape[1] // DIL_W, 3 * g
    else:
        mine = qkv16[:, 3 * g * DIL_W:3 * (g + 1) * DIL_W]
        view, nslab, first = mine.reshape(mine.shape[0] // r, r * 3 * DIL_W), 3, 0
    n_i = seq // r // Q_BLK
    assert DIL_WINDOWS[g] // r == Q_BLK and seq % (r * Q_BLK) == 0

    def spec(slab, prev):
        def imap(b, c, i):
            ii = jnp.maximum(i - 1, 0) if prev else i
            return (b * n_i + ii, c * nslab + first + slab)
        return pl.BlockSpec((Q_BLK, DIL_W), imap)

    out_spec = pl.BlockSpec((Q_BLK, DIL_W), lambda b, c, i: (b * n_i + i, c))
    shp = jax.ShapeDtypeStruct((bsz * seq // r, r * DIL_W), F32)
    o, lse = pl.pallas_call(
        _dil_prompt_kernel,
        grid=(bsz, r, n_i),
        in_specs=[spec(0, False), spec(1, True), spec(1, False), spec(2, True), spec(2, False)],
        out_specs=[out_spec, out_spec],
        out_shape=[shp, shp],
        compiler_params=_cp(("parallel", "parallel", "arbitrary")),
        name=f"dil_prompt_g{g}",
    )(view, view, view, view, view)
    return o.reshape(bsz * seq, DIL_W), lse.reshape(bsz * seq, DIL_W)


def _dil_merge_kernel(o0, o1, o2, l0, l1, l2, out_ref):
    a, b, c = l0[...], l1[...], l2[...]
    m = jnp.maximum(jnp.maximum(a, b), c)
    wa, wb, wc = jnp.exp(a - m), jnp.exp(b - m), jnp.exp(c - m)
    out = (wa * o0[...] + wb * o1[...] + wc * o2[...]) / (wa + wb + wc)
    out_ref[...] = out.astype(BF16)


def dil_merge(os_, lses, *, tm=1024):
    n = os_[0].shape[0]
    spec = pl.BlockSpec((tm, DIL_W), lambda i: (i, 0))
    return pl.pallas_call(
        _dil_merge_kernel,
        grid=(n // tm,),
        in_specs=[spec] * 6,
        out_specs=spec,
        out_shape=jax.ShapeDtypeStruct((n, DIL_W), BF16),
        compiler_params=_cp(("parallel",)),
        name="dil_merge",
    )(*os_, *lses)


NEW_PAD = 8


def _dil_sample_kernel(*refs, n_new, n_alias):
    qn_ref, c0, c1, c2 = refs[:4]
    o_ref, u0, u1, u2, e0, e1, e2, en_ref, st_ref = refs[4 + n_alias:]
    caches, ups, es = (c0, c1, c2), (u0, u1, u2), (e0, e1, e2)
    kv = pl.program_id(1)
    rows = n_new * DIL_HEADS
    off = NEW_PAD - n_new
    bd = jnp.where((_iota((DIL_HEADS, DIL_W), 1) >> 6) == _iota((DIL_HEADS, DIL_W), 0), 1.0, 0.0)
    t_row = _iota((rows, 1), 0) >> 3
    lane = _iota((DIL_W, LANES), 1)

    def shifted(x, new8):
        w = x.shape[1]
        new_t = jnp.concatenate([jnp.zeros((LANES - NEW_PAD, DIL_W), F32), new8], axis=0).T
        rolled = pltpu.roll(x, w - n_new, 1)
        tail = jnp.where(lane >= LANES - n_new, new_t, rolled[:, w - LANES:])
        return tail if w == LANES else jnp.concatenate([rolled[:, :w - LANES], tail], axis=1)

    @pl.when(kv == 0)
    def _():
        for g in range(3):
            rate = DIL_RATES[g]
            base = g * 3 * DIL_W
            qg = qn_ref[:, base:base + DIL_W] * ATTN_SCALE
            kn = qn_ref[:, base + DIL_W:base + 2 * DIL_W]
            qh = jnp.concatenate([qg[off + t:off + t + 1, :] * bd for t in range(n_new)], axis=0).astype(BF16)
            x = caches[g][...]
            w = x.shape[1]
            s_past = _dot(qh, x.astype(BF16))
            gap = _iota((rows, w), 1) - t_row
            ok_past = (gap >= 0) & ((gap & (rate - 1)) == 0)
            s_new = _nt(qh, kn.astype(BF16))
            gap_n = t_row - (_iota((rows, NEW_PAD), 1) - off)
            ok_new = (gap_n >= 0) & (gap_n <= t_row) & ((gap_n & (rate - 1)) == 0)
            s_past = jnp.where(ok_past, s_past, NEG_INF)
            s_new = jnp.where(ok_new, s_new, NEG_INF)
            m = jnp.maximum(jnp.max(s_past, axis=-1, keepdims=True), jnp.max(s_new, axis=-1, keepdims=True))
            e_past = jnp.where(ok_past, jnp.exp(s_past - m), 0.0)
            e_new = jnp.where(ok_new, jnp.exp(s_new - m), 0.0)
            den = jnp.sum(e_past, axis=-1, keepdims=True) + jnp.sum(e_new, axis=-1, keepdims=True)
            es[g][...] = e_past.astype(BF16)
            en_ref[g] = e_new
            st_ref[g, 0] = m + jnp.log(den)
            st_ref[g, 1] = 1.0 / den
            ups[g][...] = shifted(x, kn)

    @pl.when(kv == 1)
    def _():
        outs, lses = [], []
        for g in range(3):
            base = g * 3 * DIL_W
            vn = qn_ref[:, base + 2 * DIL_W:base + 3 * DIL_W]
            x = caches[g][...]
            o = _nt(es[g][...], x.astype(BF16))
            e_new = en_ref[g].astype(BF16).astype(F32)
            vn_r = vn.astype(BF16).astype(F32)
            for t in range(n_new):
                o = o + e_new[:, off + t:off + t + 1] * vn_r[off + t:off + t + 1, :]
            outs.append(o * st_ref[g, 1])
            lses.append(st_ref[g, 0])
            ups[g][...] = shifted(x, vn)
        mm = jnp.maximum(jnp.maximum(lses[0], lses[1]), lses[2])
        ws = [jnp.exp(l - mm) for l in lses]
        tot = (ws[0] * outs[0] + ws[1] * outs[1] + ws[2] * outs[2]) / (ws[0] + ws[1] + ws[2])
        tot = tot * jnp.concatenate([bd] * n_new, axis=0)
        o_ref[...] = jnp.concatenate(
            [jnp.sum(tot[t * DIL_HEADS:(t + 1) * DIL_HEADS], axis=0, keepdims=True) for t in range(n_new)], axis=0)


def dil_sample(qkv_new8, caches_t, prev, layer, *, n_new):
    nb, _, ncol = qkv_new8.shape
    for g, c in enumerate(caches_t):
        assert c.shape[4] == DIL_WINDOWS[g] and DIL_RATES[g] & (DIL_RATES[g] - 1) == 0
    rows = n_new * DIL_HEADS
    cspec = lambda c: pl.BlockSpec((None, None, None, DIL_W, c.shape[4]), lambda b, kv: (layer, b, kv, 0, 0))
    n_alias = 0 if prev is None else 3
    any_spec = pl.BlockSpec(memory_space=pl.ANY)
    outs = pl.pallas_call(
        functools.partial(_dil_sample_kernel, n_new=n_new, n_alias=n_alias),
        grid=(nb, 2),
        in_specs=[pl.BlockSpec((None, NEW_PAD, ncol), lambda b, kv: (b, 0, 0))]
                 + [cspec(c) for c in caches_t] + [any_spec] * n_alias,
        out_specs=[pl.BlockSpec((None, n_new, DIL_W), lambda b, kv: (b, 0, 0))] + [cspec(c) for c in caches_t],
        out_shape=[jax.ShapeDtypeStruct((nb, n_new, DIL_W), F32)]
                  + [jax.ShapeDtypeStruct(c.shape, F32) for c in caches_t],
        scratch_shapes=[pltpu.VMEM((rows, c.shape[4]), BF16) for c in caches_t]
                       + [pltpu.VMEM((3, rows, NEW_PAD), F32), pltpu.VMEM((3, 2, rows, 1), F32)],
        input_output_aliases={} if prev is None else {4 + g: 1 + g for g in range(3)},
        compiler_params=_cp(("parallel", "arbitrary"), VMEM_LIMIT),
        name="dil_sample",
    )(qkv_new8, *caches_t, *(() if prev is None else prev))
    return outs[0], tuple(outs[1:])


def _gather_kernel(pt_ref, *refs):
    out_ref = refs[-1]
    page = refs[0].shape[1]
    for p, ref in enumerate(refs[:-1]):
        out_ref[:, p * page:(p + 1) * page] = ref[...]


def gather_pages(cache_t, page_table, layer):
    nb, npg = page_table.shape
    w, page = cache_t.shape[2], cache_t.shape[3]

    def spec(p):
        return pl.BlockSpec((None, None, w, page), lambda b, pt: (layer, pt[b, p], 0, 0))

    grid_spec = pltpu.PrefetchScalarGridSpec(
        num_scalar_prefetch=1, grid=(nb,),
        in_specs=[spec(p) for p in range(npg)],
        out_specs=pl.BlockSpec((None, w, npg * page), lambda b, pt: (b, 0, 0)))
    return pl.pallas_call(
        _gather_kernel, grid_spec=grid_spec,
        out_shape=jax.ShapeDtypeStruct((nb, w, npg * page), cache_t.dtype),
        compiler_params=_cp(("parallel",)),
        name="gather_pages",
    )(page_table, *([cache_t] * npg))


def _gelu_tanh(x):
    return x * (0.5 * (1.0 + jnp.tanh(np.sqrt(2.0 / np.pi) * (x + 0.044715 * (x * x * x)))))


def _compress_kernel(xk_ref, xv_ref, pe_ref, w1_ref, w2_ref, k_ref, v_ref, *scratch, n_chunk):
    outs = (k_ref, v_ref)
    if scratch:
        for x_ref, xs_ref in zip((xk_ref, xv_ref), scratch):
            n_rows = x_ref.shape[2]
            for j in range(x_ref.shape[0]):
                xs_ref[j * n_rows:(j + 1) * n_rows, :] = x_ref[j].T
        xk_ref, xv_ref = scratch
    for s, x_ref in enumerate((xk_ref, xv_ref)):
        top = jnp.zeros((n_chunk, 2 * CMP_HIDDEN), F32)
        bot = jnp.zeros((n_chunk, 2 * CMP_HIDDEN), F32)
        for l in range(CMP_STRIDE):
            r = x_ref[pl.ds(l, n_chunk, stride=CMP_STRIDE), :]
            top = top + _dot((r + pe_ref[s, l:l + 1, :]).astype(BF16), w1_ref[s, l])
            lb = CMP_STRIDE + l
            bot = bot + _dot((r + pe_ref[s, lb:lb + 1, :]).astype(BF16), w1_ref[s, lb])
        pre = top + pltpu.roll(bot, n_chunk - 1, 0)
        outs[s][...] = _dot(_gelu_tanh(pre).astype(BF16), w2_ref[s])


def compress(x2d, col_blk, n_steps, pe_pair, w1bd, w2bd, *, rows=8192):
    n_chunk = rows // CMP_STRIDE
    out = jax.ShapeDtypeStruct((n_steps * n_chunk, LANES), F32)
    ospec = pl.BlockSpec((n_chunk, LANES), lambda i: (i, 0))
    if x2d.ndim == 3:
        per = rows // x2d.shape[2]
        xspecs = [pl.BlockSpec((per, LANES, x2d.shape[2]), lambda i: (i, 2 * col_blk, 0)),
                  pl.BlockSpec((per, LANES, x2d.shape[2]), lambda i: (i, 2 * col_blk + 1, 0))]
        scratch = [pltpu.VMEM((rows, LANES), F32)] * 2
    else:
        xspecs = [pl.BlockSpec((rows, LANES), lambda i: (i, 2 * col_blk)),
                  pl.BlockSpec((rows, LANES), lambda i: (i, 2 * col_blk + 1))]
        scratch = []
    return pl.pallas_call(
        functools.partial(_compress_kernel, n_chunk=n_chunk),
        grid=(n_steps,),
        in_specs=xspecs + [
                  pl.BlockSpec(pe_pair.shape, lambda i: (0, 0, 0)),
                  pl.BlockSpec(w1bd.shape, lambda i: (0, 0, 0, 0)),
                  pl.BlockSpec(w2bd.shape, lambda i: (0, 0, 0))],
        out_specs=[ospec, ospec],
        out_shape=[out, out],
        scratch_shapes=scratch,
        compiler_params=_cp(("parallel",), VMEM_LIMIT),
        name="compress",
    )(x2d, x2d, pe_pair, w1bd, w2bd)


SEL_TILE = 512
WIN_SPAN = SLD_WINDOW + Q_BLK


def _overlap(n_cmp, n_blk):
    n = _iota((n_cmp, n_blk), 0) * CMP_STRIDE
    j = _iota((n_cmp, n_blk), 1) * SEL_LEN
    return jnp.where(n < j + SEL_LEN, jnp.where(n + CMP_LEN - 1 >= j, 1.0, 0.0), 0.0)


def _block_scores(imp, qpos):
    j = _iota(imp.shape, 1)
    cur = qpos >> 6
    eligible = j <= cur
    forced = (j == 0) | (j == cur) | (j == cur - 1)
    score = jnp.where(eligible, jnp.where(forced, FORCED_SCORE, imp), NEG_INF)
    return score, eligible


def _overlap_t(n_cmp):
    j = _iota((LANES, n_cmp), 0) * SEL_LEN
    n = _iota((LANES, n_cmp), 1) * CMP_STRIDE
    return jnp.where(n < j + SEL_LEN, jnp.where(n + CMP_LEN - 1 >= j, 1.0, 0.0), 0.0)


def _rank_select(sct, eligible_t):
    n_t = LANES // 8
    tiles = [sct[8 * v:8 * v + 8, :] for v in range(n_t)]
    cnts = [jnp.zeros((8, Q_BLK), F32) for _ in range(n_t)]
    sub = _iota((8, Q_BLK), 0)
    for i2 in range(LANES):
        row = jnp.broadcast_to(sct[i2:i2 + 1, :], (8, Q_BLK))
        u = i2 // 8
        for v in range(n_t):
            if u < v:
                ahead = row >= tiles[v]
            elif u > v:
                ahead = row > tiles[v]
            else:
                ahead = (row > tiles[v]) | ((row == tiles[v]) & (sub > i2 % 8))
            cnts[v] = cnts[v] + jnp.where(ahead, 1.0, 0.0)
    cnt = jnp.concatenate(cnts, axis=0)
    return jnp.where(eligible_t, jnp.where(cnt < SEL_TOPK, 1.0, 0.0), 0.0)


def _row_chunks(n):
    return [(a, a + Q_BLK) for a in range(0, n, Q_BLK)]


def _nsa_prompt_kernel(q_ref, kc_ref, vct_ref, ks_ref, vst_ref, kw_ref, vwt_ref, gate_ref, o_ref,
                       qst_ref, sc_ref, p_ref, ps_ref, m_ref, l_ref, al_ref, acc_ref, part_ref):
    i = pl.program_id(1)
    start = i * Q_BLK
    qpos = start + _iota((1, Q_BLK), 1)
    gate_t = (1.0 / (1.0 + jnp.exp(-gate_ref[...]))).T
    n_cmp = kc_ref.shape[0]
    kc = kc_ref[...].astype(BF16)
    vct = vct_ref[...].astype(BF16)
    ov_t = _overlap_t(n_cmp)
    blk_t = _iota((LANES, Q_BLK), 0)
    cur = qpos >> 6
    eligible_t = blk_t <= cur
    forced_t = (blk_t == 0) | (blk_t == cur) | (blk_t == cur - 1)
    ws = pl.multiple_of(jnp.maximum(i - SLD_WINDOW // Q_BLK, 0) * Q_BLK, Q_BLK)
    sub_iota = _iota((Q_BLK, Q_BLK), 0)
    zero = jnp.zeros((Q_BLK, HEAD_DIM), BF16)
    cols = [slice(r * Q_BLK, (r + 1) * Q_BLK) for r in range(NSA_REP)]
    neg_row = jnp.full((1, Q_BLK), NEG_INF, F32)
    n_kt = (start + Q_BLK - 1) // SEL_TILE + 1

    for g in range(2):
        for r in range(NSA_REP):
            h = g * NSA_REP + r
            qh = q_ref[:, h * HEAD_DIM:(h + 1) * HEAD_DIM]
            qst_ref[r * Q_BLK:(r + 1) * Q_BLK, 0:LANES] = jnp.concatenate([qh, zero] if g == 0 else [zero, qh], axis=1)

        sc_ref[0, 0:n_cmp, :] = _nt(kc, qst_ref[:, 0:LANES])
        ps_ref[...] = jnp.zeros_like(ps_ref)
        for r, c in enumerate(cols):
            ok = [((a + sub_iota) * CMP_STRIDE + (CMP_LEN - 1)) <= qpos for a, _ in _row_chunks(n_cmp)]
            m = neg_row
            for k, (a, b) in enumerate(_row_chunks(n_cmp)):
                m = jnp.maximum(m, jnp.max(jnp.where(ok[k], sc_ref[0, a:b, c], NEG_INF), axis=0, keepdims=True))
            den = jnp.zeros((1, Q_BLK), F32)
            for k, (a, b) in enumerate(_row_chunks(n_cmp)):
                e = jnp.where(ok[k], jnp.exp2(sc_ref[0, a:b, c] - m), 0.0)
                den = den + jnp.sum(e, axis=0, keepdims=True)
                sc_ref[0, a:b, c] = e
            inv = 1.0 / jnp.where(den > 0, den, 1.0)
            for a, b in _row_chunks(n_cmp):
                p = sc_ref[0, a:b, c] * inv
                ps_ref[a:b, :] = ps_ref[a:b, :] + p
                p_ref[0, a:b, c] = p.astype(BF16)
        oc_t = _dot(vct, p_ref[0, 0:n_cmp, :])
        for r, c in enumerate(cols):
            h = g * NSA_REP + r
            part_ref[:, c] = gate_t[h:h + 1, :] * oc_t[:, c]

        imp_t = jnp.dot(ov_t, ps_ref[...], preferred_element_type=F32, precision=lax.Precision.HIGHEST)
        sct = jnp.where(eligible_t, jnp.where(forced_t, FORCED_SCORE, imp_t), NEG_INF)
        sel_t = _rank_select(sct, eligible_t)
        sel_far = jnp.where((blk_t >> 1) == i, 0.0, sel_t)
        sel_bias = jnp.where(sel_far.T > 0.5, 0.0, NEG_INF).astype(BF16)
        for r in range(NSA_REP):
            qst_ref[r * Q_BLK:(r + 1) * Q_BLK, LANES:2 * LANES] = sel_bias

        own = pl.multiple_of(start, Q_BLK)
        sc_ref[0, 0:Q_BLK, :] = _nt(ks_ref[pl.ds(own, Q_BLK), :], qst_ref[:, 0:LANES])
        tri = sub_iota <= _iota((Q_BLK, Q_BLK), 1)
        for c in cols:
            s_own = jnp.where(tri, sc_ref[0, 0:Q_BLK, c], NEG_INF)
            m_own = jnp.max(s_own, axis=0, keepdims=True)
            p = jnp.exp2(s_own - m_own)
            m_ref[:, c] = m_own
            l_ref[:, c] = jnp.sum(p, axis=0, keepdims=True)
            p_ref[0, 0:Q_BLK, c] = p.astype(BF16)
        acc_ref[...] = _dot(vst_ref[:, pl.ds(own, Q_BLK)], p_ref[0, 0:Q_BLK, :])

        def sel_qk(kt, slot):
            k0 = pl.multiple_of(kt * SEL_TILE, SEL_TILE)
            ks = ks_ref[pl.ds(k0, SEL_TILE), :]
            blk = kt * (SEL_TILE // SEL_LEN) + (_iota((SEL_TILE, LANES), 0) >> 6)
            onehot = jnp.where(_iota((SEL_TILE, LANES), 1) == blk, 1.0, 0.0).astype(BF16)
            sc_ref[slot, 0:SEL_TILE, :] = _nt(jnp.concatenate([ks, onehot], axis=1), qst_ref[...])

        def sel_softmax(slot):
            for c in cols:
                m_old = m_ref[:, c]
                m_new = m_old
                for a, b in _row_chunks(SEL_TILE):
                    m_new = jnp.maximum(m_new, jnp.max(sc_ref[slot, a:b, c], axis=0, keepdims=True))
                alpha = jnp.exp2(m_old - m_new)
                ls = jnp.zeros((1, Q_BLK), F32)
                for a, b in _row_chunks(SEL_TILE):
                    p = jnp.exp2(sc_ref[slot, a:b, c] - m_new)
                    ls = ls + jnp.sum(p, axis=0, keepdims=True)
                    p_ref[slot, a:b, c] = p.astype(BF16)
                l_ref[:, c] = alpha * l_ref[:, c] + ls
                m_ref[:, c] = m_new
                al_ref[slot, :, c] = alpha

        def sel_pv(kt, slot):
            k0 = pl.multiple_of(kt * SEL_TILE, SEL_TILE)
            acc_ref[...] = (al_ref[slot] * acc_ref[...]
                            + _dot(vst_ref[:, pl.ds(k0, SEL_TILE)], p_ref[slot, 0:SEL_TILE, :]))

        def pair_body(j, carry):
            sel_qk(2 * j, 0)
            sel_qk(2 * j + 1, 1)
            sel_softmax(0)
            sel_pv(2 * j, 0)
            sel_softmax(1)
            sel_pv(2 * j + 1, 1)
            return carry

        def single_body(kt, carry):
            sel_qk(kt, 0)
            sel_softmax(0)
            sel_pv(kt, 0)
            return carry

        lax.fori_loop(0, n_kt // 2, pair_body, 0)
        lax.fori_loop(2 * (n_kt // 2), n_kt, single_body, 0)

        sc_ref[0, 0:WIN_SPAN, :] = _nt(kw_ref[pl.ds(ws, WIN_SPAN), :], qst_ref[:, 0:LANES])
        for c in cols:
            ok = []
            for a, _ in _row_chunks(WIN_SPAN):
                dist = qpos - (ws + a + sub_iota)
                ok.append((dist >= 0) & (dist <= SLD_WINDOW))
            m = neg_row
            for k, (a, b) in enumerate(_row_chunks(WIN_SPAN)):
                m = jnp.maximum(m, jnp.max(jnp.where(ok[k], sc_ref[0, a:b, c], NEG_INF), axis=0, keepdims=True))
            den = jnp.zeros((1, Q_BLK), F32)
            for k, (a, b) in enumerate(_row_chunks(WIN_SPAN)):
                e = jnp.where(ok[k], jnp.exp2(sc_ref[0, a:b, c] - m), 0.0)
                den = den + jnp.sum(e, axis=0, keepdims=True)
                p_ref[0, a:b, c] = e.astype(BF16)
            al_ref[0, :, c] = 1.0 / den
        ow_t = _dot(vwt_ref[:, pl.ds(ws, WIN_SPAN)], p_ref[0, 0:WIN_SPAN, :]) * al_ref[0]

        for r, c in enumerate(cols):
            h = g * NSA_REP + r
            tot_t = (part_ref[:, c]
                     + gate_t[NSA_HEADS + h:NSA_HEADS + h + 1, :] * (acc_ref[:, c] / l_ref[:, c])
                     + gate_t[2 * NSA_HEADS + h:2 * NSA_HEADS + h + 1, :] * ow_t[:, c])
            tot = tot_t.T
            o_ref[:, h * HEAD_DIM:(h + 1) * HEAD_DIM] = tot[:, g * HEAD_DIM:(g + 1) * HEAD_DIM].astype(BF16)


def nsa_prompt(proj16, proj32, kcmp, vcmp_t, vslc_t, vwin_t, *, bsz, seq):
    n_i = seq // Q_BLK
    n_cmp = seq // CMP_STRIDE
    assert seq >= WIN_SPAN and seq % SEL_TILE == 0 and seq // SEL_LEN <= LANES and n_cmp % Q_BLK == 0
    kspec = lambda col: pl.BlockSpec((seq, LANES), lambda b, i: (b, col // LANES))
    vtspec = pl.BlockSpec((LANES, seq), lambda b, i: (b, 0))
    wide = NSA_REP * Q_BLK
    n_sc = max(n_cmp, SEL_TILE, WIN_SPAN)
    return pl.pallas_call(
        _nsa_prompt_kernel,
        grid=(bsz, n_i),
        in_specs=[pl.BlockSpec((Q_BLK, B_Q), lambda b, i: (b * n_i + i, 0)),
                  pl.BlockSpec((n_cmp, LANES), lambda b, i: (b, 0)),
                  pl.BlockSpec((LANES, n_cmp), lambda b, i: (b, 0)),
                  kspec(B_KSLC), vtspec, kspec(B_KWIN), vtspec,
                  pl.BlockSpec((Q_BLK, LANES), lambda b, i: (b * n_i + i, B_GATE // LANES))],
        out_specs=pl.BlockSpec((Q_BLK, B_Q), lambda b, i: (b * n_i + i, 0)),
        out_shape=jax.ShapeDtypeStruct((bsz * seq, B_Q), BF16),
        scratch_shapes=[pltpu.VMEM((wide, 2 * LANES), BF16),
                        pltpu.VMEM((2, n_sc, wide), F32),
                        pltpu.VMEM((2, n_sc, wide), BF16),
                        pltpu.VMEM((n_cmp, Q_BLK), F32),
                        pltpu.VMEM((1, wide), F32),
                        pltpu.VMEM((1, wide), F32),
                        pltpu.VMEM((2, 1, wide), F32),
                        pltpu.VMEM((LANES, wide), F32),
                        pltpu.VMEM((LANES, wide), F32)],
        compiler_params=_cp(("parallel", "arbitrary"), VMEM_LIMIT),
        name="nsa_prompt",
    )(proj16, kcmp, vcmp_t, proj16, vslc_t, proj16, vwin_t, proj32)


def _nsa_sample_kernel(qpad_ref, kc_ref, vc_ref, past_ref, new_ref, win_ref, gate_ref, o_ref,
                       *, n_new, n_past):
    rows = qpad_ref.shape[0]
    half = rows // 2
    rho = _iota((rows, 1), 0)
    t_row = (rho >> 3) % n_new
    qpos = n_past + t_row
    qpad = qpad_ref[...]
    gate = 1.0 / (1.0 + jnp.exp(-gate_ref[...]))
    n_pad = new_ref.shape[0]
    tnew = _iota((rows, n_pad), 1)
    new = new_ref[...]
    new16 = new.astype(BF16)
    new_r = new16.astype(F32)

    def own_half(x):
        return jnp.where(rho < half, x[:, :HEAD_DIM], x[:, HEAD_DIM:])

    def softmax_parts(s_past, ok_past, s_new, ok_new):
        s_past = jnp.where(ok_past, s_past, NEG_INF)
        s_new = jnp.where(ok_new, s_new, NEG_INF)
        m = jnp.maximum(jnp.max(s_past, axis=-1, keepdims=True), jnp.max(s_new, axis=-1, keepdims=True))
        e_past = jnp.where(ok_past, jnp.exp2(s_past - m), 0.0)
        e_new = jnp.where(ok_new, jnp.exp2(s_new - m), 0.0)
        den = jnp.sum(e_past, axis=-1, keepdims=True) + jnp.sum(e_new, axis=-1, keepdims=True)
        return e_past.astype(BF16), e_new.astype(BF16).astype(F32), den

    def new_rows_part(e_new_r, v_new_r):
        o = jnp.zeros((rows, LANES), F32)
        for t in range(n_new):
            o = o + e_new_r[:, t:t + 1] * v_new_r[t:t + 1, :]
        return own_half(o)

    n_cmp = kc_ref.shape[0]
    ok_c = (_iota((rows, n_cmp), 1) * CMP_STRIDE + (CMP_LEN - 1)) <= qpos
    s = jnp.where(ok_c, _nt(qpad, kc_ref[...].astype(BF16)), NEG_INF)
    m = jnp.max(s, axis=-1, keepdims=True)
    e = jnp.where(ok_c, jnp.exp2(s - m), 0.0)
    den = jnp.sum(e, axis=-1, keepdims=True)
    p = e / jnp.where(den > 0, den, 1.0)
    o_cmp = own_half(_dot(p.astype(BF16), vc_ref[...].astype(BF16)))

    same = jnp.where((_iota((rows, rows), 0) >> 3) == (_iota((rows, rows), 1) >> 3), 1.0, 0.0)
    psum = jnp.dot(same, p, preferred_element_type=F32, precision=lax.Precision.HIGHEST)
    imp = jnp.dot(psum, _overlap(n_cmp, LANES), preferred_element_type=F32, precision=lax.Precision.HIGHEST)
    score, eligible = _block_scores(imp, qpos)
    n_sel = -(-(n_past + n_new) // SEL_LEN)
    jcol = _iota(score.shape, 1)
    cnt = jnp.zeros(score.shape, F32)
    for i2 in range(n_sel):
        col = score[:, i2:i2 + 1]
        cnt = cnt + jnp.where(col > score, 1.0, jnp.where(col == score, jnp.where(jcol > i2, 1.0, 0.0), 0.0))
    sel = jnp.where(eligible, jnp.where(cnt < SEL_TOPK, 1.0, 0.0), 0.0)

    expand = jnp.where(_iota((LANES, n_past), 0) == (_iota((LANES, n_past), 1) >> 6), 1.0, 0.0).astype(BF16)
    selk = _dot(sel.astype(BF16), expand)
    past16 = past_ref[...].astype(BF16)
    s_past = jnp.concatenate(
        [_dot(qpad[g * half:(g + 1) * half, g * HEAD_DIM:(g + 1) * HEAD_DIM], past16[g * HEAD_DIM:(g + 1) * HEAD_DIM, :])
         for g in range(2)], axis=0)
    blk_new = n_past // SEL_LEN
    ok_new_sel = (sel[:, blk_new:blk_new + 1] > 0.5) & (tnew <= t_row)
    e_past, e_new_r, den = softmax_parts(s_past, selk > 0.5, _nt(qpad, new16[:, :LANES]), ok_new_sel)
    o_past = jnp.concatenate(
        [_nt(e_past[g * half:(g + 1) * half, :], past16[LANES + g * HEAD_DIM:LANES + (g + 1) * HEAD_DIM, :])
         for g in range(2)], axis=0)
    o_sel = (o_past + new_rows_part(e_new_r, new_r[:, LANES:2 * LANES])) / den

    win16 = win_ref[...].astype(BF16)
    n_win = win16.shape[0]
    ok_w = _iota((rows, n_win), 1) >= t_row + (n_win - SLD_WINDOW)
    e_past, e_new_r, den = softmax_parts(_nt(qpad, win16[:, :LANES]), ok_w,
                                         _nt(qpad, new16[:, 2 * LANES:3 * LANES]), tnew <= t_row)
    o_win = (own_half(_dot(e_past, win16[:, LANES:])) + new_rows_part(e_new_r, new_r[:, 3 * LANES:])) / den

    o_ref[...] = gate[:, 0:1] * o_cmp + gate[:, 1:2] * o_sel + gate[:, 2:3] * o_win


def nsa_sample(qpad, kcmp, vcmp, past_t, newkv, win_cache, gates, layer, *, n_new, n_past):
    nb, rows, _ = qpad.shape
    n_cmp = n_past // CMP_STRIDE
    n_win = win_cache.shape[2]
    assert n_past % SEL_LEN == 0 and n_new <= SEL_LEN and n_win >= SLD_WINDOW
    return pl.pallas_call(
        functools.partial(_nsa_sample_kernel, n_new=n_new, n_past=n_past),
        grid=(nb,),
        in_specs=[pl.BlockSpec((None, rows, LANES), lambda b: (b, 0, 0)),
                  pl.BlockSpec((n_cmp, LANES), lambda b: (b, 0)),
                  pl.BlockSpec((n_cmp, LANES), lambda b: (b, 0)),
                  pl.BlockSpec((None, 2 * LANES, n_past), lambda b: (b, 1, 0)),
                  pl.BlockSpec((None,) + newkv.shape[1:], lambda b: (b, 0, 0)),
                  pl.BlockSpec((None, None, n_win, 2 * LANES), lambda b: (layer, b, 0, 0)),
                  pl.BlockSpec((None, rows, LANES), lambda b: (b, 0, 0))],
        out_specs=pl.BlockSpec((None, rows, HEAD_DIM), lambda b: (b, 0, 0)),
        out_shape=jax.ShapeDtypeStruct((nb, rows, HEAD_DIM), F32),
        compiler_params=_cp(("parallel",), VMEM_LIMIT),
        name="nsa_sample",
    )(qpad, kcmp, vcmp, past_t, newkv, win_cache, gates)


def _rope_tables(pos):
    inv = ROPE_THETA ** (-jnp.arange(HALF, dtype=F32) / HALF)
    ang = pos.astype(F32)[:, None] * inv[None, :]
    cos, sin, zero = jnp.cos(ang), jnp.sin(ang), jnp.zeros_like(ang)
    return (jnp.concatenate([cos, cos] * 2, axis=1),
            jnp.concatenate([-sin, zero] * 2, axis=1),
            jnp.concatenate([zero, sin] * 2, axis=1))


def _column_roles_a():
    slab = (np.arange(3 * 3 * DIL_W) % (3 * DIL_W)) // DIL_W
    rope = (slab < 2).astype(np.float32)
    scale = np.where(slab == 0, ATTN_SCALE, 1.0).astype(np.float32)
    return rope[None, :], scale[None, :]


def _column_roles_b():
    col = np.arange(B_COLS)
    is_q = col < B_Q
    is_k = (col >= B_KCMP) & (col < B_GATE) & ((col - B_KCMP) % (2 * LANES) < LANES)
    rope = (is_q | is_k).astype(np.float32)
    scale = np.where(is_q, ATTN_SCALE * LOG2E, 1.0).astype(np.float32)
    return rope[None, :], scale[None, :]


def _block_diag2(w):
    z = jnp.zeros_like(w)
    return jnp.concatenate([jnp.concatenate([w, z], axis=-1), jnp.concatenate([z, w], axis=-1)], axis=-2)


def kernel(x_prompt, x_sample, p_prompt, p_sample, cache_a_w128, cache_a_w512, cache_a_w2048, cache_b_kv, cache_b_win, page_table, norm_mix, norm_ffn, norm_ple, norm_final, a_w_in, a_w_out, b_w_in, b_w_out, b_cmp_pe, b_cmp_w1, b_cmp_w2, moe_w_group, moe_b_group, moe_w_expert, moe_b_expert, moe_w_gate, moe_w_up, moe_w_down, ple_w_proj, ple_w_gate):
    bsz, seq, d = x_prompt.shape
    nb, n_new, _ = x_sample.shape
    depth = norm_mix.shape[0]
    n_pages, page = page_table.shape[1], cache_b_kv.shape[2]
    n_past = n_pages * page
    np_ = bsz * seq
    ns = nb * n_new

    h = jnp.concatenate([x_prompt.reshape(np_, d), x_sample.reshape(ns, d)], axis=0)
    pos = jnp.concatenate([jnp.tile(jnp.arange(seq, dtype=jnp.int32), bsz),
                           jnp.tile(n_past + jnp.arange(n_new, dtype=jnp.int32), nb)])
    rope = _rope_tables(pos)
    roles_a = tuple(jnp.asarray(a) for a in _column_roles_a())
    roles_b = tuple(jnp.asarray(a) for a in _column_roles_b())

    caches_t = [jnp.transpose(c, (0, 1, 3, 4, 5, 2)).reshape(c.shape[0], nb, 2, DIL_W, c.shape[2])
                for c in (cache_a_w128, cache_a_w512, cache_a_w2048)]
    a_upd = None
    kv_w = cache_b_kv.shape[3] * cache_b_kv.shape[4] * cache_b_kv.shape[5]
    cache_kv_t = jnp.transpose(cache_b_kv, (0, 1, 3, 4, 5, 2)).reshape(cache_b_kv.shape[0], cache_b_kv.shape[1], kv_w, page)
    cache_win4 = cache_b_win.reshape(cache_b_win.shape[0], nb, cache_b_win.shape[2], 2 * LANES)

    a_bufs_p, b_rows_p, b_rows_s, b_win_p, b_win_s = [], [], [], [], []
    for i in range(depth):
        li = i // 2
        if i % 2 == 0:
            qkv32, qkv16 = norm_proj(h, norm_mix[i], a_w_in[li].astype(BF16), *roles_a, *rope, tn=3 * DIL_W)
            parts = [dil_prompt(qkv16, g, bsz=bsz, seq=seq) for g in range(3)]
            o_p = dil_merge([p[0] for p in parts], [p[1] for p in parts])
            qkv_new8 = jnp.pad(qkv32[np_:].reshape(nb, n_new, qkv32.shape[1]), ((0, 0), (NEW_PAD - n_new, 0), (0, 0)))
            o_s, a_upd = dil_sample(qkv_new8, caches_t, a_upd, li, n_new=n_new)
            o16 = jnp.concatenate([o_p, o_s.reshape(ns, DIL_W).astype(BF16)], axis=0)
            h = out_proj(h, o16, a_w_out[li].astype(BF16))
            bp = []
            for g in range(3):
                lo, hi = g * 3 * DIL_W + DIL_W, (g + 1) * 3 * DIL_W
                keep = min(DIL_WINDOWS[g], seq)
                tail = jnp.stack([qkv32[(b + 1) * seq - keep:(b + 1) * seq, lo:hi] for b in range(bsz)], axis=0)
                bp.append(tail.reshape(bsz, keep, 2, DIL_HEADS, HEAD_DIM))
            a_bufs_p.append(bp)
        else:
            w_in = jnp.pad(b_w_in[li], ((0, 0), (0, B_COLS - b_w_in.shape[2]))).astype(BF16)
            p32, p16 = norm_proj(h, norm_mix[i], w_in, *roles_b, *rope, tn=B_COLS // 2)
            pe_pair = jnp.concatenate([b_cmp_pe[li], b_cmp_pe[li]], axis=-1)
            w1bd = _block_diag2(b_cmp_w1[li].reshape(2, CMP_LEN, HEAD_DIM, CMP_HIDDEN)).astype(BF16)
            w2bd = _block_diag2(b_cmp_w2[li]).astype(BF16)
            kc_p, vc_p = compress(p32, B_KCMP // (2 * LANES), bsz, pe_pair, w1bd, w2bd, rows=seq)
            n_cmp = seq // CMP_STRIDE
            vc_t = vc_p.reshape(bsz, n_cmp, LANES).transpose(0, 2, 1).reshape(bsz * LANES, n_cmp)
            vs_t = p16[:np_, B_VSLC:B_VSLC + LANES].reshape(bsz, seq, LANES).transpose(0, 2, 1).reshape(bsz * LANES, seq)
            vw_t = p16[:np_, B_VWIN:B_VWIN + LANES].reshape(bsz, seq, LANES).transpose(0, 2, 1).reshape(bsz * LANES, seq)
            o_p = nsa_prompt(p16, p32, kc_p, vc_t, vs_t, vw_t, bsz=bsz, seq=seq)
            past_t = gather_pages(cache_kv_t, page_table, li)
            step_rows = max(n_past, 8192)
            kc_s, vc_s = compress(past_t, 0, nb * n_past // step_rows, pe_pair, w1bd, w2bd, rows=step_rows)
            ps32, ps16 = p32[np_:], p16[np_:]
            q = ps16[:, :B_Q].reshape(nb, n_new, 2, NSA_REP, HEAD_DIM).transpose(0, 2, 1, 3, 4)
            zq = jnp.zeros_like(q[:, 0])
            qpad = jnp.stack([jnp.concatenate([q[:, 0], zq], axis=-1),
                              jnp.concatenate([zq, q[:, 1]], axis=-1)], axis=1)
            qpad = qpad.reshape(nb, 2 * n_new * NSA_REP, LANES)
            gts = ps32[:, B_GATE:B_GATE + 3 * NSA_HEADS].reshape(nb, n_new, 3, 2, NSA_REP)
            gts = gts.transpose(0, 3, 1, 4, 2).reshape(nb, 2 * n_new * NSA_REP, 3)
            gts = jnp.pad(gts, ((0, 0), (0, 0), (0, LANES - 3)))
            newkv = ps32[:, B_KSLC:B_GATE].reshape(nb, n_new, B_GATE - B_KSLC)
            newkv = jnp.pad(newkv, ((0, 0), (0, 8 - n_new), (0, 0)))
            o_s = nsa_sample(qpad, kc_s, vc_s, past_t, newkv, cache_win4, gts, li, n_new=n_new, n_past=n_past)
            o_s = o_s.reshape(nb, 2, n_new, NSA_REP, HEAD_DIM).transpose(0, 2, 1, 3, 4).reshape(ns, B_Q)
            o16 = jnp.concatenate([o_p, o_s.astype(BF16)], axis=0)
            h = out_proj(h, o16, b_w_out[li].astype(BF16))
            keep = min(SLD_WINDOW, seq)
            b_rows_p.append(p32[:np_, B_KCMP:B_KWIN].reshape(bsz, seq, 4, 2, HEAD_DIM))
            win_tail = jnp.stack([p32[(b + 1) * seq - keep:(b + 1) * seq, B_KWIN:B_GATE] for b in range(bsz)], axis=0)
            b_win_p.append(win_tail.reshape(bsz, keep, 2, 2, HEAD_DIM))
            b_rows_s.append(ps32[:, B_KCMP:B_KWIN].reshape(nb, n_new, 4, 2, HEAD_DIM))
            new_win = ps32[:, B_KWIN:B_GATE].reshape(nb, n_new, 2, 2, HEAD_DIM)
            b_win_s.append(jnp.concatenate([cache_b_win[li][:, n_new:], new_win], axis=1))

        w_route = jnp.pad(jnp.concatenate([moe_w_group[i], moe_w_expert[i]], axis=1),
                          ((0, 0), (0, LANES - N_GROUPS - N_EXPERTS)))
        b_route = jnp.pad(jnp.concatenate([moe_b_group[i], moe_b_expert[i]]),
                          (0, LANES - N_GROUPS - N_EXPERTS)).reshape(1, LANES)
        h = moe(h, norm_ffn[i], w_route, b_route, moe_w_gate[i].astype(BF16),
                moe_w_up[i].astype(BF16), moe_w_down[i].astype(BF16))
        p_tok = jnp.concatenate([p_prompt[i].reshape(np_, -1), p_sample[i].reshape(ns, -1)], axis=0)
        h = ple(h, norm_ple[i], p_tok, ple_w_gate[i].astype(BF16), ple_w_proj[i].astype(BF16))

    y = rms_norm(h, norm_final)
    a_state = [jnp.transpose(u.reshape(u.shape[0], nb, 2, DIL_HEADS, HEAD_DIM, u.shape[4]), (0, 1, 5, 2, 3, 4))
               for u in a_upd]
    stack = lambda bufs, g: jnp.stack([b[g] for b in bufs], axis=0)
    return (y[:np_].reshape(bsz, seq, d), y[np_:].reshape(nb, n_new, d),
            stack(a_bufs_p, 0), stack(a_bufs_p, 1), stack(a_bufs_p, 2),
            jnp.stack(b_rows_p, axis=0), jnp.stack(b_win_p, axis=0),
            *a_state,
            jnp.stack(b_rows_s, axis=0), jnp.stack(b_win_s, axis=0))
```

```python
import functools

import numpy as np
import jax
import jax.numpy as jnp
from jax import lax
from jax.experimental import pallas as pl
from jax.experimental.pallas import tpu as pltpu

F32 = jnp.float32
BF16 = jnp.bfloat16

HEAD_DIM = 64
HALF = HEAD_DIM // 2
ROPE_THETA = 10000.0
RMS_EPS = 1e-6
NEG_INF = -1e30
ATTN_SCALE = HEAD_DIM ** -0.5
LOG2E = 1.4426950408889634
LANES = 128
Q_BLK = 128

DIL_WINDOWS = (128, 512, 2048)
DIL_RATES = (1, 4, 16)
DIL_HEADS = 8
DIL_W = DIL_HEADS * HEAD_DIM

NSA_HEADS = 16
NSA_REP = 8
CMP_LEN = 32
CMP_STRIDE = 16
CMP_HIDDEN = 256
SEL_LEN = 64
SEL_TOPK = 16
SLD_WINDOW = 512
FORCED_SCORE = 1e9
B_COLS = 2048
B_Q = 1024
B_KCMP, B_VCMP, B_KSLC, B_VSLC, B_KWIN, B_VWIN, B_GATE = (1024, 1152, 1280, 1408, 1536, 1664, 1792)

N_GROUPS = 4
EXP_PER_GROUP = 8
N_EXPERTS = 32

VMEM_LIMIT = 56 * 1024 * 1024


def _cp(sem, vmem=None):
    return pltpu.CompilerParams(dimension_semantics=sem, vmem_limit_bytes=vmem)


def _nt(a, b):
    return lax.dot_general(a, b, (((1,), (1,)), ((), ())), preferred_element_type=F32)


def _dot(a, b):
    return jnp.dot(a, b, preferred_element_type=F32)


def _iota(shape, dim):
    return lax.broadcasted_iota(jnp.int32, shape, dim)


def _norm_proj_kernel(x_ref, g_ref, w_ref, rc_ref, sc_ref, c_ref, s1_ref, s2_ref,
                      o32_ref, o16_ref, xn_ref, *, tn):
    @pl.when(pl.program_id(1) == 0)
    def _():
        x = x_ref[...]
        var = jnp.mean(x * x, axis=-1, keepdims=True)
        xn_ref[...] = (x * lax.rsqrt(var + RMS_EPS) * g_ref[...]).astype(BF16)

    y = _dot(xn_ref[...], w_ref[...])
    reps = tn // LANES
    rc = rc_ref[...]
    c = 1.0 + rc * (jnp.tile(c_ref[...], (1, reps)) - 1.0)
    s1 = rc * jnp.tile(s1_ref[...], (1, reps))
    s2 = rc * jnp.tile(s2_ref[...], (1, reps))
    y = y * c + pltpu.roll(y, tn - HALF, 1) * s1 + pltpu.roll(y, HALF, 1) * s2
    o32_ref[...] = y
    o16_ref[...] = (y * sc_ref[...]).astype(BF16)


def norm_proj(x, g, w16, ropecol, colscale, rope_c, rope_s1, rope_s2, *, tm=512, tn=512):
    n, d = x.shape
    ncols = w16.shape[1]
    assert n % tm == 0 and ncols % tn == 0
    return pl.pallas_call(
        functools.partial(_norm_proj_kernel, tn=tn),
        grid=(n // tm, ncols // tn),
        in_specs=[
            pl.BlockSpec((tm, d), lambda i, j: (i, 0)),
            pl.BlockSpec((1, d), lambda i, j: (0, 0)),
            pl.BlockSpec((d, tn), lambda i, j: (0, j)),
            pl.BlockSpec((1, tn), lambda i, j: (0, j)),
            pl.BlockSpec((1, tn), lambda i, j: (0, j)),
            pl.BlockSpec((tm, LANES), lambda i, j: (i, 0)),
            pl.BlockSpec((tm, LANES), lambda i, j: (i, 0)),
            pl.BlockSpec((tm, LANES), lambda i, j: (i, 0)),
        ],
        out_specs=[pl.BlockSpec((tm, tn), lambda i, j: (i, j)),
                   pl.BlockSpec((tm, tn), lambda i, j: (i, j))],
        out_shape=[jax.ShapeDtypeStruct((n, ncols), F32),
                   jax.ShapeDtypeStruct((n, ncols), BF16)],
        scratch_shapes=[pltpu.VMEM((tm, d), BF16)],
        compiler_params=_cp(("parallel", "arbitrary")),
        name="norm_proj",
    )(x, g.reshape(1, d), w16, ropecol, colscale, rope_c, rope_s1, rope_s2)


def _out_proj_kernel(h_ref, o_ref, w_ref, out_ref):
    out_ref[...] = h_ref[...] + _dot(o_ref[...], w_ref[...])


def out_proj(h, o16, w16, *, tm=512):
    n, d = h.shape
    k = o16.shape[1]
    return pl.pallas_call(
        _out_proj_kernel,
        grid=(n // tm,),
        in_specs=[pl.BlockSpec((tm, d), lambda i: (i, 0)),
                  pl.BlockSpec((tm, k), lambda i: (i, 0)),
                  pl.BlockSpec((k, d), lambda i: (0, 0))],
        out_specs=pl.BlockSpec((tm, d), lambda i: (i, 0)),
        out_shape=jax.ShapeDtypeStruct((n, d), F32),
        compiler_params=_cp(("parallel",)),
        name="out_proj",
    )(h, o16, w16)


def _moe_kernel(h_ref, g_ref, wr_ref, br_ref, wg_ref, wu_ref, wd_ref, out_ref,
                xn_ref, comb_ref, acc_ref):
    grp = pl.program_id(1)

    @pl.when(grp == 0)
    def _():
        x = h_ref[...]
        var = jnp.mean(x * x, axis=-1, keepdims=True)
        xn = x * lax.rsqrt(var + RMS_EPS) * g_ref[...]
        xn_ref[...] = xn.astype(BF16)
        logit = jnp.dot(xn, wr_ref[...], preferred_element_type=F32,
                        precision=lax.Precision.HIGHEST) + br_ref[...]
        lane = _iota(logit.shape, 1)
        is_grp = lane < N_GROUPS
        lg = jnp.where(is_grp, logit, -jnp.inf)
        gmax = jnp.max(lg, axis=-1, keepdims=True)
        gidx = jnp.min(jnp.where(lg == gmax, lane, LANES), axis=-1, keepdims=True)
        p_grp = 1.0 / jnp.sum(jnp.where(is_grp, jnp.exp(logit - gmax), 0.0), axis=-1, keepdims=True)
        lo = N_GROUPS + gidx * EXP_PER_GROUP
        in_grp = (lane >= lo) & (lane < lo + EXP_PER_GROUP)
        le = jnp.where(in_grp, logit, -jnp.inf)
        v1 = jnp.max(le, axis=-1, keepdims=True)
        i1 = jnp.min(jnp.where(le == v1, lane, LANES), axis=-1, keepdims=True)
        le2 = jnp.where(lane == i1, -jnp.inf, le)
        v2 = jnp.max(le2, axis=-1, keepdims=True)
        i2 = jnp.min(jnp.where(le2 == v2, lane, LANES), axis=-1, keepdims=True)
        e2 = jnp.exp(v2 - v1)
        w1 = p_grp / (1.0 + e2)
        w2 = p_grp * e2 / (1.0 + e2)
        comb = jnp.where(lane == i1, w1, jnp.where(lane == i2, w2, 0.0))
        for gg in range(N_GROUPS):
            sel = jnp.where(gidx == gg, comb, 0.0)
            comb_ref[gg] = pltpu.roll(sel, LANES - (N_GROUPS + gg * EXP_PER_GROUP), 1)
        acc_ref[...] = jnp.zeros_like(acc_ref)

    xn = xn_ref[...]
    comb = comb_ref[grp]
    acc = acc_ref[...]
    for k in range(EXP_PER_GROUP):
        hg = _dot(xn, wg_ref[k])
        hu = _dot(xn, wu_ref[k])
        a = (hg / (1.0 + jnp.exp(-hg))) * hu * comb[:, k:k + 1]
        acc = acc + _dot(a.astype(BF16), wd_ref[k])
    acc_ref[...] = acc

    @pl.when(grp == N_GROUPS - 1)
    def _():
        out_ref[...] = h_ref[...] + acc_ref[...]


def moe(h, g, w_route, b_route, wg16, wu16, wd16, *, tm=512):
    n, d = h.shape
    f = wg16.shape[2]
    return pl.pallas_call(
        _moe_kernel,
        grid=(n // tm, N_GROUPS),
        in_specs=[
            pl.BlockSpec((tm, d), lambda i, e: (i, 0)),
            pl.BlockSpec((1, d), lambda i, e: (0, 0)),
            pl.BlockSpec((d, LANES), lambda i, e: (0, 0)),
            pl.BlockSpec((1, LANES), lambda i, e: (0, 0)),
            pl.BlockSpec((EXP_PER_GROUP, d, f), lambda i, e: (e, 0, 0)),
            pl.BlockSpec((EXP_PER_GROUP, d, f), lambda i, e: (e, 0, 0)),
            pl.BlockSpec((EXP_PER_GROUP, f, d), lambda i, e: (e, 0, 0)),
        ],
        out_specs=pl.BlockSpec((tm, d), lambda i, e: (i, 0)),
        out_shape=jax.ShapeDtypeStruct((n, d), F32),
        scratch_shapes=[pltpu.VMEM((tm, d), BF16),
                        pltpu.VMEM((N_GROUPS, tm, LANES), F32),
                        pltpu.VMEM((tm, d), F32)],
        compiler_params=_cp(("parallel", "arbitrary"), VMEM_LIMIT),
        name="moe",
    )(h, g.reshape(1, d), w_route, b_route, wg16, wu16, wd16)


def _ple_kernel(h_ref, g_ref, p_ref, wgate_ref, wproj_ref, out_ref):
    x = h_ref[...]
    var = jnp.mean(x * x, axis=-1, keepdims=True)
    xn = (x * lax.rsqrt(var + RMS_EPS) * g_ref[...]).astype(BF16)
    gate = 1.0 / (1.0 + jnp.exp(-_dot(xn, wgate_ref[...])))
    out_ref[...] = x + gate * _dot(p_ref[...].astype(BF16), wproj_ref[...])


def ple(h, g, p, wgate16, wproj16, *, tm=512):
    n, d = h.shape
    pd = p.shape[1]
    return pl.pallas_call(
        _ple_kernel,
        grid=(n // tm,),
        in_specs=[pl.BlockSpec((tm, d), lambda i: (i, 0)),
                  pl.BlockSpec((1, d), lambda i: (0, 0)),
                  pl.BlockSpec((tm, pd), lambda i: (i, 0)),
                  pl.BlockSpec((d, d), lambda i: (0, 0)),
                  pl.BlockSpec((pd, d), lambda i: (0, 0))],
        out_specs=pl.BlockSpec((tm, d), lambda i: (i, 0)),
        out_shape=jax.ShapeDtypeStruct((n, d), F32),
        compiler_params=_cp(("parallel",)),
        name="ple",
    )(h, g.reshape(1, d), p, wgate16, wproj16)


def _rms_kernel(h_ref, g_ref, out_ref):
    x = h_ref[...]
    var = jnp.mean(x * x, axis=-1, keepdims=True)
    out_ref[...] = x * lax.rsqrt(var + RMS_EPS) * g_ref[...]


def rms_norm(h, g, *, tm=512):
    n, d = h.shape
    return pl.pallas_call(
        _rms_kernel,
        grid=(n // tm,),
        in_specs=[pl.BlockSpec((tm, d), lambda i: (i, 0)),
                  pl.BlockSpec((1, d), lambda i: (0, 0))],
        out_specs=pl.BlockSpec((tm, d), lambda i: (i, 0)),
        out_shape=jax.ShapeDtypeStruct((n, d), F32),
        compiler_params=_cp(("parallel",)),
        name="final_norm",
    )(h, g.reshape(1, d))


def _dil_prompt_kernel(q_ref, kp_ref, kc_ref, vp_ref, vc_ref, o_ref, lse_ref):
    i = pl.program_id(2)
    q = q_ref[...]
    kk = jnp.concatenate([kp_ref[...], kc_ref[...]], axis=0)
    vv = jnp.concatenate([vp_ref[...], vc_ref[...]], axis=0)
    qu = i * Q_BLK + _iota((Q_BLK, 2 * Q_BLK), 0)
    ku = (i - 1) * Q_BLK + _iota((Q_BLK, 2 * Q_BLK), 1)
    dist = qu - ku
    valid = (ku >= 0) & (dist >= 0) & (dist <= Q_BLK)
    low = _iota((Q_BLK, LANES), 1) < HEAD_DIM
    for hp in range(DIL_HEADS // 2):
        sl = slice(hp * LANES, (hp + 1) * LANES)
        qp, kp, vp = q[:, sl], kk[:, sl], vv[:, sl]
        halves = []
        for mine in (low, ~low):
            s = jnp.where(valid, _nt(jnp.where(mine, qp, jnp.zeros_like(qp)), kp), NEG_INF)
            m = jnp.max(s, axis=-1, keepdims=True)
            e = jnp.exp(s - m)
            den = jnp.sum(e, axis=-1, keepdims=True)
            halves.append((_dot(e.astype(BF16), vp) / den, m + jnp.log(den)))
        o_ref[:, sl] = jnp.where(low, halves[0][0], halves[1][0])
        lse_ref[:, sl] = jnp.where(low, halves[0][1], halves[1][1])


def dil_prompt(qkv16, g, *, bsz, seq):
    r = DIL_RATES[g]
    if r == 1:
        view, nslab, first = qkv16, qkv16.shape[1] // DIL_W, 3 * g
    else:
        mine = qkv16[:, 3 * g * DIL_W:3 * (g + 1) * DIL_W]
        view, nslab, first = mine.reshape(mine.shape[0] // r, r * 3 * DIL_W), 3, 0
    n_i = seq // r // Q_BLK
    assert DIL_WINDOWS[g] // r == Q_BLK and seq % (r * Q_BLK) == 0

    def spec(slab, prev):
        def imap(b, c, i):
            ii = jnp.maximum(i - 1, 0) if prev else i
            return (b * n_i + ii, c * nslab + first + slab)
        return pl.BlockSpec((Q_BLK, DIL_W), imap)

    out_spec = pl.BlockSpec((Q_BLK, DIL_W), lambda b, c, i: (b * n_i + i, c))
    shp = jax.ShapeDtypeStruct((bsz * seq // r, r * DIL_W), F32)
    o, lse = pl.pallas_call(
        _dil_prompt_kernel,
        grid=(bsz, r, n_i),
        in_specs=[spec(0, False), spec(1, True), spec(1, False), spec(2, True), spec(2, False)],
        out_specs=[out_spec, out_spec],
        out_shape=[shp, shp],
        compiler_params=_cp(("parallel", "parallel", "arbitrary")),
        name=f"dil_prompt_g{g}",
    )(view, view, view, view, view)
    return o.reshape(bsz * seq, DIL_W), lse.reshape(bsz * seq, DIL_W)


def _dil_merge_kernel(o0, o1, o2, l0, l1, l2, out_ref):
    a, b, c = l0[...], l1[...], l2[...]
    m = jnp.maximum(jnp.maximum(a, b), c)
    wa, wb, wc = jnp.exp(a - m), jnp.exp(b - m), jnp.exp(c - m)
    out = (wa * o0[...] + wb * o1[...] + wc * o2[...]) / (wa + wb + wc)
    out_ref[...] = out.astype(BF16)


def dil_merge(os_, lses, *, tm=1024):
    n = os_[0].shape[0]
    spec = pl.BlockSpec((tm, DIL_W), lambda i: (i, 0))
    return pl.pallas_call(
        _dil_merge_kernel,
        grid=(n // tm,),
        in_specs=[spec] * 6,
        out_specs=spec,
        out_shape=jax.ShapeDtypeStruct((n, DIL_W), BF16),
        compiler_params=_cp(("parallel",)),
        name="dil_merge",
    )(*os_, *lses)


NEW_PAD = 8


def _dil_sample_kernel(*refs, n_new, n_alias):
    qn_ref, c0, c1, c2 = refs[:4]
    o_ref, u0, u1, u2, e0, e1, e2, en_ref, st_ref = refs[4 + n_alias:]
    caches, ups, es = (c0, c1, c2), (u0, u1, u2), (e0, e1, e2)
    kv = pl.program_id(1)
    rows = n_new * DIL_HEADS
    off = NEW_PAD - n_new
    bd = jnp.where((_iota((DIL_HEADS, DIL_W), 1) >> 6) == _iota((DIL_HEADS, DIL_W), 0), 1.0, 0.0)
    t_row = _iota((rows, 1), 0) >> 3
    lane = _iota((DIL_W, LANES), 1)

    def shifted(x, new8):
        w = x.shape[1]
        new_t = jnp.concatenate([jnp.zeros((LANES - NEW_PAD, DIL_W), F32), new8], axis=0).T
        rolled = pltpu.roll(x, w - n_new, 1)
        tail = jnp.where(lane >= LANES - n_new, new_t, rolled[:, w - LANES:])
        return tail if w == LANES else jnp.concatenate([rolled[:, :w - LANES], tail], axis=1)

    @pl.when(kv == 0)
    def _():
        for g in range(3):
            rate = DIL_RATES[g]
            base = g * 3 * DIL_W
            qg = qn_ref[:, base:base + DIL_W] * ATTN_SCALE
            kn = qn_ref[:, base + DIL_W:base + 2 * DIL_W]
            qh = jnp.concatenate([qg[off + t:off + t + 1, :] * bd for t in range(n_new)], axis=0).astype(BF16)
            x = caches[g][...]
            w = x.shape[1]
            s_past = _dot(qh, x.astype(BF16))
            gap = _iota((rows, w), 1) - t_row
            ok_past = (gap >= 0) & ((gap & (rate - 1)) == 0)
            s_new = _nt(qh, kn.astype(BF16))
            gap_n = t_row - (_iota((rows, NEW_PAD), 1) - off)
            ok_new = (gap_n >= 0) & (gap_n <= t_row) & ((gap_n & (rate - 1)) == 0)
            s_past = jnp.where(ok_past, s_past, NEG_INF)
            s_new = jnp.where(ok_new, s_new, NEG_INF)
            m = jnp.maximum(jnp.max(s_past, axis=-1, keepdims=True), jnp.max(s_new, axis=-1, keepdims=True))
            e_past = jnp.where(ok_past, jnp.exp(s_past - m), 0.0)
            e_new = jnp.where(ok_new, jnp.exp(s_new - m), 0.0)
            den = jnp.sum(e_past, axis=-1, keepdims=True) + jnp.sum(e_new, axis=-1, keepdims=True)
            es[g][...] = e_past.astype(BF16)
            en_ref[g] = e_new
            st_ref[g, 0] = m + jnp.log(den)
            st_ref[g, 1] = 1.0 / den
            ups[g][...] = shifted(x, kn)

    @pl.when(kv == 1)
    def _():
        outs, lses = [], []
        for g in range(3):
            base = g * 3 * DIL_W
            vn = qn_ref[:, base + 2 * DIL_W:base + 3 * DIL_W]
            x = caches[g][...]
            o = _nt(es[g][...], x.astype(BF16))
            e_new = en_ref[g].astype(BF16).astype(F32)
            vn_r = vn.astype(BF16).astype(F32)
            for t in range(n_new):
                o = o + e_new[:, off + t:off + t + 1] * vn_r[off + t:off + t + 1, :]
            outs.append(o * st_ref[g, 1])
            lses.append(st_ref[g, 0])
            ups[g][...] = shifted(x, vn)
        mm = jnp.maximum(jnp.maximum(lses[0], lses[1]), lses[2])
        ws = [jnp.exp(l - mm) for l in lses]
        tot = (ws[0] * outs[0] + ws[1] * outs[1] + ws[2] * outs[2]) / (ws[0] + ws[1] + ws[2])
        tot = tot * jnp.concatenate([bd] * n_new, axis=0)
        o_ref[...] = jnp.concatenate(
            [jnp.sum(tot[t * DIL_HEADS:(t + 1) * DIL_HEADS], axis=0, keepdims=True) for t in range(n_new)], axis=0)


def dil_sample(qkv_new8, caches_t, prev, layer, *, n_new):
    nb, _, ncol = qkv_new8.shape
    for g, c in enumerate(caches_t):
        assert c.shape[4] == DIL_WINDOWS[g] and DIL_RATES[g] & (DIL_RATES[g] - 1) == 0
    rows = n_new * DIL_HEADS
    cspec = lambda c: pl.BlockSpec((None, None, None, DIL_W, c.shape[4]), lambda b, kv: (layer, b, kv, 0, 0))
    n_alias = 0 if prev is None else 3
    any_spec = pl.BlockSpec(memory_space=pl.ANY)
    outs = pl.pallas_call(
        functools.partial(_dil_sample_kernel, n_new=n_new, n_alias=n_alias),
        grid=(nb, 2),
        in_specs=[pl.BlockSpec((None, NEW_PAD, ncol), lambda b, kv: (b, 0, 0))]
                 + [cspec(c) for c in caches_t] + [any_spec] * n_alias,
        out_specs=[pl.BlockSpec((None, n_new, DIL_W), lambda b, kv: (b, 0, 0))] + [cspec(c) for c in caches_t],
        out_shape=[jax.ShapeDtypeStruct((nb, n_new, DIL_W), F32)]
                  + [jax.ShapeDtypeStruct(c.shape, F32) for c in caches_t],
        scratch_shapes=[pltpu.VMEM((rows, c.shape[4]), BF16) for c in caches_t]
                       + [pltpu.VMEM((3, rows, NEW_PAD), F32), pltpu.VMEM((3, 2, rows, 1), F32)],
        input_output_aliases={} if prev is None else {4 + g: 1 + g for g in range(3)},
        compiler_params=_cp(("parallel", "arbitrary"), VMEM_LIMIT),
        name="dil_sample",
    )(qkv_new8, *caches_t, *(() if prev is None else prev))
    return outs[0], tuple(outs[1:])


def _gather_kernel(pt_ref, *refs):
    out_ref = refs[-1]
    page = refs[0].shape[1]
    for p, ref in enumerate(refs[:-1]):
        out_ref[:, p * page:(p + 1) * page] = ref[...]


def gather_pages(cache_t, page_table, layer):
    nb, npg = page_table.shape
    w, page = cache_t.shape[2], cache_t.shape[3]

    def spec(p):
        return pl.BlockSpec((None, None, w, page), lambda b, pt: (layer, pt[b, p], 0, 0))

    grid_spec = pltpu.PrefetchScalarGridSpec(
        num_scalar_prefetch=1, grid=(nb,),
        in_specs=[spec(p) for p in range(npg)],
        out_specs=pl.BlockSpec((None, w, npg * page), lambda b, pt: (b, 0, 0)))
    return pl.pallas_call(
        _gather_kernel, grid_spec=grid_spec,
        out_shape=jax.ShapeDtypeStruct((nb, w, npg * page), cache_t.dtype),
        compiler_params=_cp(("parallel",)),
        name="gather_pages",
    )(page_table, *([cache_t] * npg))


def _gelu_tanh(x):
    return x * (0.5 * (1.0 + jnp.tanh(np.sqrt(2.0 / np.pi) * (x + 0.044715 * (x * x * x)))))


def _compress_kernel(xk_ref, xv_ref, pe_ref, w1_ref, w2_ref, k_ref, v_ref, *scratch, n_chunk):
    outs = (k_ref, v_ref)
    if scratch:
        for x_ref, xs_ref in zip((xk_ref, xv_ref), scratch):
            n_rows = x_ref.shape[2]
            for j in range(x_ref.shape[0]):
                xs_ref[j * n_rows:(j + 1) * n_rows, :] = x_ref[j].T
        xk_ref, xv_ref = scratch
    for s, x_ref in enumerate((xk_ref, xv_ref)):
        top = jnp.zeros((n_chunk, 2 * CMP_HIDDEN), F32)
        bot = jnp.zeros((n_chunk, 2 * CMP_HIDDEN), F32)
        for l in range(CMP_STRIDE):
            r = x_ref[pl.ds(l, n_chunk, stride=CMP_STRIDE), :]
            top = top + _dot((r + pe_ref[s, l:l + 1, :]).astype(BF16), w1_ref[s, l])
            lb = CMP_STRIDE + l
            bot = bot + _dot((r + pe_ref[s, lb:lb + 1, :]).astype(BF16), w1_ref[s, lb])
        pre = top + pltpu.roll(bot, n_chunk - 1, 0)
        outs[s][...] = _dot(_gelu_tanh(pre).astype(BF16), w2_ref[s])


def compress(x2d, col_blk, n_steps, pe_pair, w1bd, w2bd, *, rows=8192):
    n_chunk = rows // CMP_STRIDE
    out = jax.ShapeDtypeStruct((n_steps * n_chunk, LANES), F32)
    ospec = pl.BlockSpec((n_chunk, LANES), lambda i: (i, 0))
    if x2d.ndim == 3:
        per = rows // x2d.shape[2]
        xspecs = [pl.BlockSpec((per, LANES, x2d.shape[2]), lambda i: (i, 2 * col_blk, 0)),
                  pl.BlockSpec((per, LANES, x2d.shape[2]), lambda i: (i, 2 * col_blk + 1, 0))]
        scratch = [pltpu.VMEM((rows, LANES), F32)] * 2
    else:
        xspecs = [pl.BlockSpec((rows, LANES), lambda i: (i, 2 * col_blk)),
                  pl.BlockSpec((rows, LANES), lambda i: (i, 2 * col_blk + 1))]
        scratch = []
    return pl.pallas_call(
        functools.partial(_compress_kernel, n_chunk=n_chunk),
        grid=(n_steps,),
        in_specs=xspecs + [
                  pl.BlockSpec(pe_pair.shape, lambda i: (0, 0, 0)),
                  pl.BlockSpec(w1bd.shape, lambda i: (0, 0, 0, 0)),
                  pl.BlockSpec(w2bd.shape, lambda i: (0, 0, 0))],
        out_specs=[ospec, ospec],
        out_shape=[out, out],
        scratch_shapes=scratch,
        compiler_params=_cp(("parallel",), VMEM_LIMIT),
        name="compress",
    )(x2d, x2d, pe_pair, w1bd, w2bd)


SEL_TILE = 512
WIN_SPAN = SLD_WINDOW + Q_BLK


def _overlap(n_cmp, n_blk):
    n = _iota((n_cmp, n_blk), 0) * CMP_STRIDE
    j = _iota((n_cmp, n_blk), 1) * SEL_LEN
    return jnp.where(n < j + SEL_LEN, jnp.where(n + CMP_LEN - 1 >= j, 1.0, 0.0), 0.0)


def _block_scores(imp, qpos):
    j = _iota(imp.shape, 1)
    cur = qpos >> 6
    eligible = j <= cur
    forced = (j == 0) | (j == cur) | (j == cur - 1)
    score = jnp.where(eligible, jnp.where(forced, FORCED_SCORE, imp), NEG_INF)
    return score, eligible


def _overlap_t(n_cmp):
    j = _iota((LANES, n_cmp), 0) * SEL_LEN
    n = _iota((LANES, n_cmp), 1) * CMP_STRIDE
    return jnp.where(n < j + SEL_LEN, jnp.where(n + CMP_LEN - 1 >= j, 1.0, 0.0), 0.0)


def _rank_select(sct, eligible_t):
    n_t = LANES // 8
    tiles = [sct[8 * v:8 * v + 8, :] for v in range(n_t)]
    cnts = [jnp.zeros((8, Q_BLK), F32) for _ in range(n_t)]
    sub = _iota((8, Q_BLK), 0)
    for i2 in range(LANES):
        row = jnp.broadcast_to(sct[i2:i2 + 1, :], (8, Q_BLK))
        u = i2 // 8
        for v in range(n_t):
            if u < v:
                ahead = row >= tiles[v]
            elif u > v:
                ahead = row > tiles[v]
            else:
                ahead = (row > tiles[v]) | ((row == tiles[v]) & (sub > i2 % 8))
            cnts[v] = cnts[v] + jnp.where(ahead, 1.0, 0.0)
    cnt = jnp.concatenate(cnts, axis=0)
    return jnp.where(eligible_t, jnp.where(cnt < SEL_TOPK, 1.0, 0.0), 0.0)


def _row_chunks(n):
    return [(a, a + Q_BLK) for a in range(0, n, Q_BLK)]


def _nsa_prompt_kernel(q_ref, kc_ref, vct_ref, ks_ref, vst_ref, kw_ref, vwt_ref, gate_ref, o_ref,
                       qst_ref, sc_ref, p_ref, ps_ref, m_ref, l_ref, al_ref, acc_ref, part_ref):
    i = pl.program_id(1)
    start = i * Q_BLK
    qpos = start + _iota((1, Q_BLK), 1)
    gate_t = (1.0 / (1.0 + jnp.exp(-gate_ref[...]))).T
    n_cmp = kc_ref.shape[0]
    kc = kc_ref[...].astype(BF16)
    vct = vct_ref[...].astype(BF16)
    ov_t = _overlap_t(n_cmp)
    blk_t = _iota((LANES, Q_BLK), 0)
    cur = qpos >> 6
    eligible_t = blk_t <= cur
    forced_t = (blk_t == 0) | (blk_t == cur) | (blk_t == cur - 1)
    ws = pl.multiple_of(jnp.maximum(i - SLD_WINDOW // Q_BLK, 0) * Q_BLK, Q_BLK)
    sub_iota = _iota((Q_BLK, Q_BLK), 0)
    zero = jnp.zeros((Q_BLK, HEAD_DIM), BF16)
    cols = [slice(r * Q_BLK, (r + 1) * Q_BLK) for r in range(NSA_REP)]
    neg_row = jnp.full((1, Q_BLK), NEG_INF, F32)
    n_kt = (start + Q_BLK - 1) // SEL_TILE + 1

    for g in range(2):
        for r in range(NSA_REP):
            h = g * NSA_REP + r
            qh = q_ref[:, h * HEAD_DIM:(h + 1) * HEAD_DIM]
            qst_ref[r * Q_BLK:(r + 1) * Q_BLK, 0:LANES] = jnp.concatenate([qh, zero] if g == 0 else [zero, qh], axis=1)

        sc_ref[0, 0:n_cmp, :] = _nt(kc, qst_ref[:, 0:LANES])
        ps_ref[...] = jnp.zeros_like(ps_ref)
        for r, c in enumerate(cols):
            ok = [((a + sub_iota) * CMP_STRIDE + (CMP_LEN - 1)) <= qpos for a, _ in _row_chunks(n_cmp)]
            m = neg_row
            for k, (a, b) in enumerate(_row_chunks(n_cmp)):
                m = jnp.maximum(m, jnp.max(jnp.where(ok[k], sc_ref[0, a:b, c], NEG_INF), axis=0, keepdims=True))
            den = jnp.zeros((1, Q_BLK), F32)
            for k, (a, b) in enumerate(_row_chunks(n_cmp)):
                e = jnp.where(ok[k], jnp.exp2(sc_ref[0, a:b, c] - m), 0.0)
                den = den + jnp.sum(e, axis=0, keepdims=True)
                sc_ref[0, a:b, c] = e
            inv = 1.0 / jnp.where(den > 0, den, 1.0)
            for a, b in _row_chunks(n_cmp):
                p = sc_ref[0, a:b, c] * inv
                ps_ref[a:b, :] = ps_ref[a:b, :] + p
                p_ref[0, a:b, c] = p.astype(BF16)
        oc_t = _dot(vct, p_ref[0, 0:n_cmp, :])
        for r, c in enumerate(cols):
            h = g * NSA_REP + r
            part_ref[:, c] = gate_t[h:h + 1, :] * oc_t[:, c]

        imp_t = jnp.dot(ov_t, ps_ref[...], preferred_element_type=F32, precision=lax.Precision.HIGHEST)
        sct = jnp.where(eligible_t, jnp.where(forced_t, FORCED_SCORE, imp_t), NEG_INF)
        sel_t = _rank_select(sct, eligible_t)
        sel_far = jnp.where((blk_t >> 1) == i, 0.0, sel_t)
        sel_bias = jnp.where(sel_far.T > 0.5, 0.0, NEG_INF).astype(BF16)
        for r in range(NSA_REP):
            qst_ref[r * Q_BLK:(r + 1) * Q_BLK, LANES:2 * LANES] = sel_bias

        own = pl.multiple_of(start, Q_BLK)
        sc_ref[0, 0:Q_BLK, :] = _nt(ks_ref[pl.ds(own, Q_BLK), :], qst_ref[:, 0:LANES])
        tri = sub_iota <= _iota((Q_BLK, Q_BLK), 1)
        for c in cols:
            s_own = jnp.where(tri, sc_ref[0, 0:Q_BLK, c], NEG_INF)
            m_own = jnp.max(s_own, axis=0, keepdims=True)
            p = jnp.exp2(s_own - m_own)
            m_ref[:, c] = m_own
            l_ref[:, c] = jnp.sum(p, axis=0, keepdims=True)
            p_ref[0, 0:Q_BLK, c] = p.astype(BF16)
        acc_ref[...] = _dot(vst_ref[:, pl.ds(own, Q_BLK)], p_ref[0, 0:Q_BLK, :])

        def sel_qk(kt, slot):
            k0 = pl.multiple_of(kt * SEL_TILE, SEL_TILE)
            ks = ks_ref[pl.ds(k0, SEL_TILE), :]
            blk = kt * (SEL_TILE // SEL_LEN) + (_iota((SEL_TILE, LANES), 0) >> 6)
            onehot = jnp.where(_iota((SEL_TILE, LANES), 1) == blk, 1.0, 0.0).astype(BF16)
            sc_ref[slot, 0:SEL_TILE, :] = _nt(jnp.concatenate([ks, onehot], axis=1), qst_ref[...])

        def sel_softmax(slot):
            for c in cols:
                m_old = m_ref[:, c]
                m_new = m_old
                for a, b in _row_chunks(SEL_TILE):
                    m_new = jnp.maximum(m_new, jnp.max(sc_ref[slot, a:b, c], axis=0, keepdims=True))
                alpha = jnp.exp2(m_old - m_new)
                ls = jnp.zeros((1, Q_BLK), F32)
                for a, b in _row_chunks(SEL_TILE):
                    p = jnp.exp2(sc_ref[slot, a:b, c] - m_new)
                    ls = ls + jnp.sum(p, axis=0, keepdims=True)
                    p_ref[slot, a:b, c] = p.astype(BF16)
                l_ref[:, c] = alpha * l_ref[:, c] + ls
                m_ref[:, c] = m_new
                al_ref[slot, :, c] = alpha

        def sel_pv(kt, slot):
            k0 = pl.multiple_of(kt * SEL_TILE, SEL_TILE)
            acc_ref[...] = (al_ref[slot] * acc_ref[...]
                            + _dot(vst_ref[:, pl.ds(k0, SEL_TILE)], p_ref[slot, 0:SEL_TILE, :]))

        def pair_body(j, carry):
            sel_qk(2 * j, 0)
            sel_qk(2 * j + 1, 1)
            sel_softmax(0)
            sel_pv(2 * j, 0)
            sel_softmax(1)
            sel_pv(2 * j + 1, 1)
            return carry

        def single_body(kt, carry):
            sel_qk(kt, 0)
            sel_softmax(0)
            sel_pv(kt, 0)
            return carry

        lax.fori_loop(0, n_kt // 2, pair_body, 0)
        lax.fori_loop(2 * (n_kt // 2), n_kt, single_body, 0)

        sc_ref[0, 0:WIN_SPAN, :] = _nt(kw_ref[pl.ds(ws, WIN_SPAN), :], qst_ref[:, 0:LANES])
        for c in cols:
            ok = []
            for a, _ in _row_chunks(WIN_SPAN):
                dist = qpos - (ws + a + sub_iota)
                ok.append((dist >= 0) & (dist <= SLD_WINDOW))
            m = neg_row
            for k, (a, b) in enumerate(_row_chunks(WIN_SPAN)):
                m = jnp.maximum(m, jnp.max(jnp.where(ok[k], sc_ref[0, a:b, c], NEG_INF), axis=0, keepdims=True))
            den = jnp.zeros((1, Q_BLK), F32)
            for k, (a, b) in enumerate(_row_chunks(WIN_SPAN)):
                e = jnp.where(ok[k], jnp.exp2(sc_ref[0, a:b, c] - m), 0.0)
                den = den + jnp.sum(e, axis=0, keepdims=True)
                p_ref[0, a:b, c] = e.astype(BF16)
            al_ref[0, :, c] = 1.0 / den
        ow_t = _dot(vwt_ref[:, pl.ds(ws, WIN_SPAN)], p_ref[0, 0:WIN_SPAN, :]) * al_ref[0]

        for r, c in enumerate(cols):
            h = g * NSA_REP + r
            tot_t = (part_ref[:, c]
                     + gate_t[NSA_HEADS + h:NSA_HEADS + h + 1, :] * (acc_ref[:, c] / l_ref[:, c])
                     + gate_t[2 * NSA_HEADS + h:2 * NSA_HEADS + h + 1, :] * ow_t[:, c])
            tot = tot_t.T
            o_ref[:, h * HEAD_DIM:(h + 1) * HEAD_DIM] = tot[:, g * HEAD_DIM:(g + 1) * HEAD_DIM].astype(BF16)


def nsa_prompt(proj16, proj32, kcmp, vcmp_t, vslc_t, vwin_t, *, bsz, seq):
    n_i = seq // Q_BLK
    n_cmp = seq // CMP_STRIDE
    assert seq >= WIN_SPAN and seq % SEL_TILE == 0 and seq // SEL_LEN <= LANES and n_cmp % Q_BLK == 0
    kspec = lambda col: pl.BlockSpec((seq, LANES), lambda b, i: (b, col // LANES))
    vtspec = pl.BlockSpec((LANES, seq), lambda b, i: (b, 0))
    wide = NSA_REP * Q_BLK
    n_sc = max(n_cmp, SEL_TILE, WIN_SPAN)
    return pl.pallas_call(
        _nsa_prompt_kernel,
        grid=(bsz, n_i),
        in_specs=[pl.BlockSpec((Q_BLK, B_Q), lambda b, i: (b * n_i + i, 0)),
                  pl.BlockSpec((n_cmp, LANES), lambda b, i: (b, 0)),
                  pl.BlockSpec((LANES, n_cmp), lambda b, i: (b, 0)),
                  kspec(B_KSLC), vtspec, kspec(B_KWIN), vtspec,
                  pl.BlockSpec((Q_BLK, LANES), lambda b, i: (b * n_i + i, B_GATE // LANES))],
        out_specs=pl.BlockSpec((Q_BLK, B_Q), lambda b, i: (b * n_i + i, 0)),
        out_shape=jax.ShapeDtypeStruct((bsz * seq, B_Q), BF16),
        scratch_shapes=[pltpu.VMEM((wide, 2 * LANES), BF16),
                        pltpu.VMEM((2, n_sc, wide), F32),
                        pltpu.VMEM((2, n_sc, wide), BF16),
                        pltpu.VMEM((n_cmp, Q_BLK), F32),
                        pltpu.VMEM((1, wide), F32),
                        pltpu.VMEM((1, wide), F32),
                        pltpu.VMEM((2, 1, wide), F32),
                        pltpu.VMEM((LANES, wide), F32),
                        pltpu.VMEM((LANES, wide), F32)],
        compiler_params=_cp(("parallel", "arbitrary"), VMEM_LIMIT),
        name="nsa_prompt",
    )(proj16, kcmp, vcmp_t, proj16, vslc_t, proj16, vwin_t, proj32)


def _nsa_sample_kernel(qpad_ref, kc_ref, vc_ref, past_ref, new_ref, win_ref, gate_ref, o_ref,
                       *, n_new, n_past):
    rows = qpad_ref.shape[0]
    half = rows // 2
    rho = _iota((rows, 1), 0)
    t_row = (rho >> 3) % n_new
    qpos = n_past + t_row
    qpad = qpad_ref[...]
    gate = 1.0 / (1.0 + jnp.exp(-gate_ref[...]))
    n_pad = new_ref.shape[0]
    tnew = _iota((rows, n_pad), 1)
    new = new_ref[...]
    new16 = new.astype(BF16)
    new_r = new16.astype(F32)

    def own_half(x):
        return jnp.where(rho < half, x[:, :HEAD_DIM], x[:, HEAD_DIM:])

    def softmax_parts(s_past, ok_past, s_new, ok_new):
        s_past = jnp.where(ok_past, s_past, NEG_INF)
        s_new = jnp.where(ok_new, s_new, NEG_INF)
        m = jnp.maximum(jnp.max(s_past, axis=-1, keepdims=True), jnp.max(s_new, axis=-1, keepdims=True))
        e_past = jnp.where(ok_past, jnp.exp2(s_past - m), 0.0)
        e_new = jnp.where(ok_new, jnp.exp2(s_new - m), 0.0)
        den = jnp.sum(e_past, axis=-1, keepdims=True) + jnp.sum(e_new, axis=-1, keepdims=True)
        return e_past.astype(BF16), e_new.astype(BF16).astype(F32), den

    def new_rows_part(e_new_r, v_new_r):
        o = jnp.zeros((rows, LANES), F32)
        for t in range(n_new):
            o = o + e_new_r[:, t:t + 1] * v_new_r[t:t + 1, :]
        return own_half(o)

    n_cmp = kc_ref.shape[0]
    ok_c = (_iota((rows, n_cmp), 1) * CMP_STRIDE + (CMP_LEN - 1)) <= qpos
    s = jnp.where(ok_c, _nt(qpad, kc_ref[...].astype(BF16)), NEG_INF)
    m = jnp.max(s, axis=-1, keepdims=True)
    e = jnp.where(ok_c, jnp.exp2(s - m), 0.0)
    den = jnp.sum(e, axis=-1, keepdims=True)
    p = e / jnp.where(den > 0, den, 1.0)
    o_cmp = own_half(_dot(p.astype(BF16), vc_ref[...].astype(BF16)))

    same = jnp.where((_iota((rows, rows), 0) >> 3) == (_iota((rows, rows), 1) >> 3), 1.0, 0.0)
    psum = jnp.dot(same, p, preferred_element_type=F32, precision=lax.Precision.HIGHEST)
    imp = jnp.dot(psum, _overlap(n_cmp, LANES), preferred_element_type=F32, precision=lax.Precision.HIGHEST)
    score, eligible = _block_scores(imp, qpos)
    n_sel = -(-(n_past + n_new) // SEL_LEN)
    jcol = _iota(score.shape, 1)
    cnt = jnp.zeros(score.shape, F32)
    for i2 in range(n_sel):
        col = score[:, i2:i2 + 1]
        cnt = cnt + jnp.where(col > score, 1.0, jnp.where(col == score, jnp.where(jcol > i2, 1.0, 0.0), 0.0))
    sel = jnp.where(eligible, jnp.where(cnt < SEL_TOPK, 1.0, 0.0), 0.0)

    expand = jnp.where(_iota((LANES, n_past), 0) == (_iota((LANES, n_past), 1) >> 6), 1.0, 0.0).astype(BF16)
    selk = _dot(sel.astype(BF16), expand)
    past16 = past_ref[...].astype(BF16)
    s_past = jnp.concatenate(
        [_dot(qpad[g * half:(g + 1) * half, g * HEAD_DIM:(g + 1) * HEAD_DIM], past16[g * HEAD_DIM:(g + 1) * HEAD_DIM, :])
         for g in range(2)], axis=0)
    blk_new = n_past // SEL_LEN
    ok_new_sel = (sel[:, blk_new:blk_new + 1] > 0.5) & (tnew <= t_row)
    e_past, e_new_r, den = softmax_parts(s_past, selk > 0.5, _nt(qpad, new16[:, :LANES]), ok_new_sel)
    o_past = jnp.concatenate(
        [_nt(e_past[g * half:(g + 1) * half, :], past16[LANES + g * HEAD_DIM:LANES + (g + 1) * HEAD_DIM, :])
         for g in range(2)], axis=0)
    o_sel = (o_past + new_rows_part(e_new_r, new_r[:, LANES:2 * LANES])) / den

    win16 = win_ref[...].astype(BF16)
    n_win = win16.shape[0]
    ok_w = _iota((rows, n_win), 1) >= t_row + (n_win - SLD_WINDOW)
    e_past, e_new_r, den = softmax_parts(_nt(qpad, win16[:, :LANES]), ok_w,
                                         _nt(qpad, new16[:, 2 * LANES:3 * LANES]), tnew <= t_row)
    o_win = (own_half(_dot(e_past, win16[:, LANES:])) + new_rows_part(e_new_r, new_r[:, 3 * LANES:])) / den

    o_ref[...] = gate[:, 0:1] * o_cmp + gate[:, 1:2] * o_sel + gate[:, 2:3] * o_win


def nsa_sample(qpad, kcmp, vcmp, past_t, newkv, win_cache, gates, layer, *, n_new, n_past):
    nb, rows, _ = qpad.shape
    n_cmp = n_past // CMP_STRIDE
    n_win = win_cache.shape[2]
    assert n_past % SEL_LEN == 0 and n_new <= SEL_LEN and n_win >= SLD_WINDOW
    return pl.pallas_call(
        functools.partial(_nsa_sample_kernel, n_new=n_new, n_past=n_past),
        grid=(nb,),
        in_specs=[pl.BlockSpec((None, rows, LANES), lambda b: (b, 0, 0)),
                  pl.BlockSpec((n_cmp, LANES), lambda b: (b, 0)),
                  pl.BlockSpec((n_cmp, LANES), lambda b: (b, 0)),
                  pl.BlockSpec((None, 2 * LANES, n_past), lambda b: (b, 1, 0)),
                  pl.BlockSpec((None,) + newkv.shape[1:], lambda b: (b, 0, 0)),
                  pl.BlockSpec((None, None, n_win, 2 * LANES), lambda b: (layer, b, 0, 0)),
                  pl.BlockSpec((None, rows, LANES), lambda b: (b, 0, 0))],
        out_specs=pl.BlockSpec((None, rows, HEAD_DIM), lambda b: (b, 0, 0)),
        out_shape=jax.ShapeDtypeStruct((nb, rows, HEAD_DIM), F32),
        compiler_params=_cp(("parallel",), VMEM_LIMIT),
        name="nsa_sample",
    )(qpad, kcmp, vcmp, past_t, newkv, win_cache, gates)


def _rope_tables(pos):
    inv = ROPE_THETA ** (-jnp.arange(HALF, dtype=F32) / HALF)
    ang = pos.astype(F32)[:, None] * inv[None, :]
    cos, sin, zero = jnp.cos(ang), jnp.sin(ang), jnp.zeros_like(ang)
    return (jnp.concatenate([cos, cos] * 2, axis=1),
            jnp.concatenate([-sin, zero] * 2, axis=1),
            jnp.concatenate([zero, sin] * 2, axis=1))


def _column_roles_a():
    slab = (np.arange(3 * 3 * DIL_W) % (3 * DIL_W)) // DIL_W
    rope = (slab < 2).astype(np.float32)
    scale = np.where(slab == 0, ATTN_SCALE, 1.0).astype(np.float32)
    return rope[None, :], scale[None, :]


def _column_roles_b():
    col = np.arange(B_COLS)
    is_q = col < B_Q
    is_k = (col >= B_KCMP) & (col < B_GATE) & ((col - B_KCMP) % (2 * LANES) < LANES)
    rope = (is_q | is_k).astype(np.float32)
    scale = np.where(is_q, ATTN_SCALE * LOG2E, 1.0).astype(np.float32)
    return rope[None, :], scale[None, :]


def _block_diag2(w):
    z = jnp.zeros_like(w)
    return jnp.concatenate([jnp.concatenate([w, z], axis=-1), jnp.concatenate([z, w], axis=-1)], axis=-2)


def kernel(x_prompt, x_sample, p_prompt, p_sample, cache_a_w128, cache_a_w512, cache_a_w2048, cache_b_kv, cache_b_win, page_table, norm_mix, norm_ffn, norm_ple, norm_final, a_w_in, a_w_out, b_w_in, b_w_out, b_cmp_pe, b_cmp_w1, b_cmp_w2, moe_w_group, moe_b_group, moe_w_expert, moe_b_expert, moe_w_gate, moe_w_up, moe_w_down, ple_w_proj, ple_w_gate):
    bsz, seq, d = x_prompt.shape
    nb, n_new, _ = x_sample.shape
    depth = norm_mix.shape[0]
    n_pages, page = page_table.shape[1], cache_b_kv.shape[2]
    n_past = n_pages * page
    np_ = bsz * seq
    ns = nb * n_new

    h = jnp.concatenate([x_prompt.reshape(np_, d), x_sample.reshape(ns, d)], axis=0)
    pos = jnp.concatenate([jnp.tile(jnp.arange(seq, dtype=jnp.int32), bsz),
                           jnp.tile(n_past + jnp.arange(n_new, dtype=jnp.int32), nb)])
    rope = _rope_tables(pos)
    roles_a = tuple(jnp.asarray(a) for a in _column_roles_a())
    roles_b = tuple(jnp.asarray(a) for a in _column_roles_b())

    caches_t = [jnp.transpose(c, (0, 1, 3, 4, 5, 2)).reshape(c.shape[0], nb, 2, DIL_W, c.shape[2])
                for c in (cache_a_w128, cache_a_w512, cache_a_w2048)]
    a_upd = None
    kv_w = cache_b_kv.shape[3] * cache_b_kv.shape[4] * cache_b_kv.shape[5]
    cache_kv_t = jnp.transpose(cache_b_kv, (0, 1, 3, 4, 5, 2)).reshape(cache_b_kv.shape[0], cache_b_kv.shape[1], kv_w, page)
    cache_win4 = cache_b_win.reshape(cache_b_win.shape[0], nb, cache_b_win.shape[2], 2 * LANES)

    a_bufs_p, b_rows_p, b_rows_s, b_win_p, b_win_s = [], [], [], [], []
    for i in range(depth):
        li = i // 2
        if i % 2 == 0:
            qkv32, qkv16 = norm_proj(h, norm_mix[i], a_w_in[li].astype(BF16), *roles_a, *rope, tn=3 * DIL_W)
            parts = [dil_prompt(qkv16, g, bsz=bsz, seq=seq) for g in range(3)]
            o_p = dil_merge([p[0] for p in parts], [p[1] for p in parts])
            qkv_new8 = jnp.pad(qkv32[np_:].reshape(nb, n_new, qkv32.shape[1]), ((0, 0), (NEW_PAD - n_new, 0), (0, 0)))
            o_s, a_upd = dil_sample(qkv_new8, caches_t, a_upd, li, n_new=n_new)
            o16 = jnp.concatenate([o_p, o_s.reshape(ns, DIL_W).astype(BF16)], axis=0)
            h = out_proj(h, o16, a_w_out[li].astype(BF16))
            bp = []
            for g in range(3):
                lo, hi = g * 3 * DIL_W + DIL_W, (g + 1) * 3 * DIL_W
                keep = min(DIL_WINDOWS[g], seq)
                tail = jnp.stack([qkv32[(b + 1) * seq - keep:(b + 1) * seq, lo:hi] for b in range(bsz)], axis=0)
                bp.append(tail.reshape(bsz, keep, 2, DIL_HEADS, HEAD_DIM))
            a_bufs_p.append(bp)
        else:
            w_in = jnp.pad(b_w_in[li], ((0, 0), (0, B_COLS - b_w_in.shape[2]))).astype(BF16)
            p32, p16 = norm_proj(h, norm_mix[i], w_in, *roles_b, *rope, tn=B_COLS // 2)
            pe_pair = jnp.concatenate([b_cmp_pe[li], b_cmp_pe[li]], axis=-1)
            w1bd = _block_diag2(b_cmp_w1[li].reshape(2, CMP_LEN, HEAD_DIM, CMP_HIDDEN)).astype(BF16)
            w2bd = _block_diag2(b_cmp_w2[li]).astype(BF16)
            kc_p, vc_p = compress(p32, B_KCMP // (2 * LANES), bsz, pe_pair, w1bd, w2bd, rows=seq)
            n_cmp = seq // CMP_STRIDE
            vc_t = vc_p.reshape(bsz, n_cmp, LANES).transpose(0, 2, 1).reshape(bsz * LANES, n_cmp)
            vs_t = p16[:np_, B_VSLC:B_VSLC + LANES].reshape(bsz, seq, LANES).transpose(0, 2, 1).reshape(bsz * LANES, seq)
            vw_t = p16[:np_, B_VWIN:B_VWIN + LANES].reshape(bsz, seq, LANES).transpose(0, 2, 1).reshape(bsz * LANES, seq)
            o_p = nsa_prompt(p16, p32, kc_p, vc_t, vs_t, vw_t, bsz=bsz, seq=seq)
            past_t = gather_pages(cache_kv_t, page_table, li)
            step_rows = max(n_past, 8192)
            kc_s, vc_s = compress(past_t, 0, nb * n_past // step_rows, pe_pair, w1bd, w2bd, rows=step_rows)
            ps32, ps16 = p32[np_:], p16[np_:]
            q = ps16[:, :B_Q].reshape(nb, n_new, 2, NSA_REP, HEAD_DIM).transpose(0, 2, 1, 3, 4)
            zq = jnp.zeros_like(q[:, 0])
            qpad = jnp.stack([jnp.concatenate([q[:, 0], zq], axis=-1),
                              jnp.concatenate([zq, q[:, 1]], axis=-1)], axis=1)
            qpad = qpad.reshape(nb, 2 * n_new * NSA_REP, LANES)
            gts = ps32[:, B_GATE:B_GATE + 3 * NSA_HEADS].reshape(nb, n_new, 3, 2, NSA_REP)
            gts = gts.transpose(0, 3, 1, 4, 2).reshape(nb, 2 * n_new * NSA_REP, 3)
            gts = jnp.pad(gts, ((0, 0), (0, 0), (0, LANES - 3)))
            newkv = ps32[:, B_KSLC:B_GATE].reshape(nb, n_new, B_GATE - B_KSLC)
            newkv = jnp.pad(newkv, ((0, 0), (0, 8 - n_new), (0, 0)))
            o_s = nsa_sample(qpad, kc_s, vc_s, past_t, newkv, cache_win4, gts, li, n_new=n_new, n_past=n_past)
            o_s = o_s.reshape(nb, 2, n_new, NSA_REP, HEAD_DIM).transpose(0, 2, 1, 3, 4).reshape(ns, B_Q)
            o16 = jnp.concatenate([o_p, o_s.astype(BF16)], axis=0)
            h = out_proj(h, o16, b_w_out[li].astype(BF16))
            keep = min(SLD_WINDOW, seq)
            b_rows_p.append(p32[:np_, B_KCMP:B_KWIN].reshape(bsz, seq, 4, 2, HEAD_DIM))
            win_tail = jnp.stack([p32[(b + 1) * seq - keep:(b + 1) * seq, B_KWIN:B_GATE] for b in range(bsz)], axis=0)
            b_win_p.append(win_tail.reshape(bsz, keep, 2, 2, HEAD_DIM))
            b_rows_s.append(ps32[:, B_KCMP:B_KWIN].reshape(nb, n_new, 4, 2, HEAD_DIM))
            new_win = ps32[:, B_KWIN:B_GATE].reshape(nb, n_new, 2, 2, HEAD_DIM)
            b_win_s.append(jnp.concatenate([cache_b_win[li][:, n_new:], new_win], axis=1))

        w_route = jnp.pad(jnp.concatenate([moe_w_group[i], moe_w_expert[i]], axis=1),
                          ((0, 0), (0, LANES - N_GROUPS - N_EXPERTS)))
        b_route = jnp.pad(jnp.concatenate([moe_b_group[i], moe_b_expert[i]]),
                          (0, LANES - N_GROUPS - N_EXPERTS)).reshape(1, LANES)
        h = moe(h, norm_ffn[i], w_route, b_route, moe_w_gate[i].astype(BF16),
                moe_w_up[i].astype(BF16), moe_w_down[i].astype(BF16))
        p_tok = jnp.concatenate([p_prompt[i].reshape(np_, -1), p_sample[i].reshape(ns, -1)], axis=0)
        h = ple(h, norm_ple[i], p_tok, ple_w_gate[i].astype(BF16), ple_w_proj[i].astype(BF16))

    y = rms_norm(h, norm_final)
    a_state = [jnp.transpose(u.reshape(u.shape[0], nb, 2, DIL_HEADS, HEAD_DIM, u.shape[4]), (0, 1, 5, 2, 3, 4))
               for u in a_upd]
    stack = lambda bufs, g: jnp.stack([b[g] for b in bufs], axis=0)
    return (y[:np_].reshape(bsz, seq, d), y[np_:].reshape(nb, n_new, d),
            stack(a_bufs_p, 0), stack(a_bufs_p, 1), stack(a_bufs_p, 2),
            jnp.stack(b_rows_p, axis=0), jnp.stack(b_win_p, axis=0),
            *a_state,
            jnp.stack(b_rows_s, axis=0), jnp.stack(b_win_s, axis=0))
```
